```python
import jax, jax.numpy as jnp
from jax import lax
import numpy as np

D_MODEL = 2048
BATCH = 2
SEQ = 4096
DEPTH = 1

CHUNK = 64
GLA_HEADS = 4
GLA_DK = D_MODEL // 8
GLA_DV = D_MODEL // 4
GLA_GATE_RANK = 16
GLA_TAU = 16.0
GMLP_BLOCK = 128
GMLP_GROUPS = 8
GMLP_WIDTH = D_MODEL
GMLP_DG = GMLP_WIDTH // GMLP_GROUPS
N_BRANCH = 2
D_FF = 4 * D_MODEL
EPS = 1e-6
QK_W = GLA_HEADS * GLA_DK
V_W = GLA_HEADS * GLA_DV
SPLIT_SIZES = (QK_W, QK_W, V_W, V_W, GLA_GATE_RANK, GMLP_WIDTH, GMLP_WIDTH, N_BRANCH * D_MODEL)
D_IN = QK_W + QK_W + V_W + V_W + GLA_GATE_RANK + GMLP_WIDTH + GMLP_WIDTH + N_BRANCH * D_MODEL

kernel_name = "hybrid_gla_gmlp_gated_block"


def rmsnorm(x, w):
    xf = x.astype(jnp.float32)
    y = xf * lax.rsqrt(jnp.mean(xf * xf, axis=-1, keepdims=True) + EPS)
    return (y * w.astype(jnp.float32)).astype(x.dtype)


def gla_branch(q, k, v, gate_lr, r, w_alpha_up, b_alpha, gla_norm_w):
    B, S, _ = q.shape
    N = S // CHUNK
    f32 = jnp.float32

    def heads(t, d):
        return t.reshape(B, N, CHUNK, GLA_HEADS, d).transpose(0, 3, 1, 2, 4)

    log_a = jax.nn.log_sigmoid((gate_lr @ w_alpha_up + b_alpha).astype(f32)) / GLA_TAU
    qh = heads(q.astype(f32), GLA_DK) * (GLA_DK ** -0.5)
    kh = heads(k.astype(f32), GLA_DK)
    vh = heads(v.astype(f32), GLA_DV)
    lcum = jnp.cumsum(heads(log_a, GLA_DK), axis=3)
    l_end = lcum[:, :, :, -1:]
    k_dec = kh * jnp.exp(l_end - lcum)
    scores = jnp.einsum('bhncd,bhnsd->bhncs', qh, k_dec)
    o_intra = jnp.einsum('bhncs,bhnse->bhnce', scores, vh)
    kv = jnp.einsum('bhnsd,bhnse->bhnde', k_dec, vh)
    q_inter = qh * jnp.exp(l_end)
    decay = jnp.exp(l_end[:, :, :, 0])

    def step(state, xs):
        q_c, kv_c, d_c = xs
        o = jnp.einsum('bhcd,bhde->bhce', q_c, state)
        state = d_c[..., None] * state + kv_c
        return state, o

    xs = (jnp.moveaxis(q_inter, 2, 0), jnp.moveaxis(kv, 2, 0), jnp.moveaxis(decay, 2, 0))
    state0 = jnp.zeros((B, GLA_HEADS, GLA_DK, GLA_DV), f32)
    _, o_inter = lax.scan(step, state0, xs)
    o = o_intra + jnp.moveaxis(o_inter, 0, 2)
    o = o * lax.rsqrt(jnp.mean(o * o, axis=-1, keepdims=True) + EPS) * gla_norm_w.astype(f32)
    o = o.transpose(0, 2, 3, 1, 4).reshape(B, S, V_W)
    return (o * jax.nn.silu(r.astype(f32))).astype(q.dtype)


def gmlp_branch(u, v, ln_w, ln_b, w_spatial, b_spatial):
    B, S, _ = u.shape
    f32 = jnp.float32
    nb = S // GMLP_BLOCK
    u = jax.nn.gelu(u.astype(f32)).reshape(B, nb, GMLP_BLOCK, GMLP_GROUPS, GMLP_DG)
    v = jax.nn.gelu(v.astype(f32)).reshape(B, nb, GMLP_BLOCK, GMLP_GROUPS, GMLP_DG)
    mu = jnp.mean(v, axis=-1, keepdims=True)
    vc = v - mu
    v = vc * lax.rsqrt(jnp.mean(vc * vc, axis=-1, keepdims=True) + EPS) * ln_w + ln_b
    pos_chunk = jnp.arange(GMLP_BLOCK) // CHUNK
    mask = pos_chunk[:, None] >= pos_chunk[None, :]
    w = jnp.where(mask[None], w_spatial.astype(f32), 0.0)
    mixed = jnp.einsum('gts,bnsgc->bntgc', w, v) + b_spatial.astype(f32).T[None, None, :, :, None]
    return (u * mixed).reshape(B, S, GMLP_WIDTH).astype(u.dtype)


def setup_inputs(seed: int = 0) -> dict:
    key = jax.random.key(seed)
    ks = jax.random.split(key, 20)
    n = jax.random.normal
    L = DEPTH
    f32 = jnp.float32
    return {
        "x": n(ks[0], (BATCH, SEQ, D_MODEL), f32),
        "norm_mix_w": 1.0 + 0.01 * n(ks[1], (L, D_MODEL), f32),
        "w_in": n(ks[2], (L, D_MODEL, D_IN), f32) * D_MODEL ** -0.5,
        "w_alpha_up": n(ks[3], (L, GLA_GATE_RANK, QK_W), f32) * GLA_GATE_RANK ** -0.5,
        "b_alpha": 0.1 * n(ks[4], (L, QK_W), f32),
        "gla_norm_w": 1.0 + 0.01 * n(ks[5], (L, GLA_DV), f32),
        "gmlp_ln_w": 1.0 + 0.01 * n(ks[6], (L, GMLP_DG), f32),
        "gmlp_ln_b": 0.01 * n(ks[7], (L, GMLP_DG), f32),
        "w_spatial": n(ks[8], (L, GMLP_GROUPS, GMLP_BLOCK, GMLP_BLOCK), f32) * GMLP_BLOCK ** -0.5,
        "b_spatial": 1.0 + 0.01 * n(ks[9], (L, GMLP_GROUPS, GMLP_BLOCK), f32),
        "b_gate": 0.01 * n(ks[10], (L, N_BRANCH, D_MODEL), f32),
        "w_branch": n(ks[11], (L, N_BRANCH, V_W, D_MODEL), f32) * V_W ** -0.5,
        "w_out": n(ks[12], (L, D_MODEL, D_MODEL), f32) * D_MODEL ** -0.5,
        "norm_mlp_w": 1.0 + 0.01 * n(ks[13], (L, D_MODEL), f32),
        "w_ff_up": n(ks[14], (L, D_MODEL, D_FF), f32) * D_MODEL ** -0.5,
        "w_ff_down": n(ks[15], (L, D_FF, D_MODEL), f32) * D_FF ** -0.5,
        "norm_final_w": 1.0 + 0.01 * n(ks[16], (D_MODEL,), f32),
    }


def reference(x, norm_mix_w, w_in, w_alpha_up, b_alpha, gla_norm_w, gmlp_ln_w, gmlp_ln_b,
              w_spatial, b_spatial, b_gate, w_branch, w_out, norm_mlp_w, w_ff_up,
              w_ff_down, norm_final_w):
    B, S, _ = x.shape
    split_idx = [int(i) for i in np.cumsum(SPLIT_SIZES)[:-1]]
    h = x
    for l in range(DEPTH):
        xn = rmsnorm(h, norm_mix_w[l])
        proj = xn @ w_in[l]
        q, k, v, r, glr, gu, gv, gates = jnp.split(proj, split_idx, axis=-1)
        o_gla = gla_branch(q, k, v, glr, r, w_alpha_up[l], b_alpha[l], gla_norm_w[l])
        o_gmlp = gmlp_branch(gu, gv, gmlp_ln_w[l], gmlp_ln_b[l], w_spatial[l], b_spatial[l])
        branches = jnp.stack([o_gla, o_gmlp], axis=2)
        branch_d = jnp.einsum('bsnc,ncd->bsnd', branches, w_branch[l])
        g = jax.nn.sigmoid(gates.reshape(B, S, N_BRANCH, D_MODEL) + b_gate[l])
        mixed = jnp.sum(g * branch_d, axis=2)
        h = h + mixed @ w_out[l]
        hn = rmsnorm(h, norm_mlp_w[l])
        h = h + jnp.square(jax.nn.relu(hn @ w_ff_up[l])) @ w_ff_down[l]
    return rmsnorm(h, norm_final_w)
```

```python
import functools

import jax
import jax.numpy as jnp
from jax import lax
from jax.experimental import pallas as pl
from jax.experimental.pallas import tpu as pltpu

D_MODEL = 2048
CHUNK = 64
GLA_HEADS = 4
GLA_DK = D_MODEL // 8
GLA_DV = D_MODEL // 4
GLA_GATE_RANK = 16
GLA_TAU = 16.0
GMLP_BLOCK = 128
GMLP_GROUPS = 8
GMLP_DG = D_MODEL // GMLP_GROUPS
D_FF = 4 * D_MODEL
EPS = 1e-6
QK_W = GLA_HEADS * GLA_DK
V_W = GLA_HEADS * GLA_DV

F32 = jnp.float32
BF16 = jnp.bfloat16

COL_Q = 0
COL_K = QK_W
COL_V = 2 * QK_W
COL_R = COL_V + V_W
COL_GU = COL_R + V_W
COL_GV = COL_GU + D_MODEL
COL_GATE = COL_GV + D_MODEL
P_WIDTH = COL_GATE + 2 * D_MODEL

V7X_VMEM_BYTES = 64 * 1024 * 1024
VMEM_LIMIT = V7X_VMEM_BYTES - 8 * 1024 * 1024

IN_TM = 1024
IN_TN = 1024
IN_RC = 256
GLA_TB = 512
MERGE_TM = 256
FFN_TM = 512
FFN_TF = 1024


def _sigmoid(x):
    return 0.5 * (1.0 + jnp.tanh(0.5 * x))


def _gelu(x):
    return jax.nn.gelu(x, approximate=True)


def _in_proj_kernel(x_ref, nw_ref, w_ref, wg_ref, bg_ref, lnw_ref, lnb_ref,
                    p_ref, glr_ref, xn_ref):
    j = pl.program_id(1)
    n_chunks = IN_TM // IN_RC

    @pl.when(j == 0)
    def _():
        for c in range(n_chunks):
            rows = pl.ds(c * IN_RC, IN_RC)
            x = x_ref[rows, :]
            ms = jnp.mean(x * x, axis=-1, keepdims=True)
            xn = (x * lax.rsqrt(ms + EPS) * nw_ref[...]).astype(BF16)
            xn_ref[rows, :] = xn
            glr_ref[rows, :] = jnp.dot(xn, wg_ref[...], preferred_element_type=F32)

    def run(epilogue):
        for c in range(n_chunks):
            rows = pl.ds(c * IN_RC, IN_RC)
            acc = jnp.dot(xn_ref[rows, :], w_ref[...], preferred_element_type=F32)
            p_ref[rows, :] = epilogue(acc).astype(BF16)

    def epi_ln(acc):
        outs = []
        for g in range(IN_TN // GMLP_DG):
            cols = slice(g * GMLP_DG, (g + 1) * GMLP_DG)
            v = _gelu(acc[:, cols])
            mu = jnp.mean(v, axis=-1, keepdims=True)
            vc = v - mu
            var = jnp.mean(vc * vc, axis=-1, keepdims=True)
            outs.append(vc * lax.rsqrt(var + EPS) * lnw_ref[...] + lnb_ref[...])
        return jnp.concatenate(outs, axis=-1)

    jq, jk, jv, jr, jgu, jgv, jgate = (c // IN_TN for c in
                                       (COL_Q, COL_K, COL_V, COL_R, COL_GU, COL_GV, COL_GATE))

    @pl.when(j == jq)
    def _():
        run(lambda a: a * (GLA_DK ** -0.5))

    @pl.when(jnp.logical_and(j >= jk, j < jr))
    def _():
        run(lambda a: a)

    @pl.when(jnp.logical_and(j >= jr, j < jgu))
    def _():
        run(lambda a: a * _sigmoid(a))

    @pl.when(jnp.logical_and(j >= jgu, j < jgv))
    def _():
        run(_gelu)

    @pl.when(jnp.logical_and(j >= jgv, j < jgate))
    def _():
        run(epi_ln)

    @pl.when(j >= jgate)
    def _():
        run(lambda a: _sigmoid(a + bg_ref[...]))


def _in_proj(x2, norm_w, w_main, w_glr, b_gate, ln_w, ln_b):
    m = x2.shape[0]
    n_gate0 = COL_GATE // IN_TN
    n_gate_blocks = (2 * D_MODEL) // IN_TN
    grid = (m // IN_TM, P_WIDTH // IN_TN)
    return pl.pallas_call(
        _in_proj_kernel,
        grid=grid,
        in_specs=[
            pl.BlockSpec((IN_TM, D_MODEL), lambda i, j: (i, 0)),
            pl.BlockSpec((1, D_MODEL), lambda i, j: (0, 0)),
            pl.BlockSpec((D_MODEL, IN_TN), lambda i, j: (0, j)),
            pl.BlockSpec((D_MODEL, GLA_GATE_RANK), lambda i, j: (0, 0)),
            pl.BlockSpec((1, IN_TN),
                         lambda i, j: (0, jnp.clip(j - n_gate0, 0, n_gate_blocks - 1))),
            pl.BlockSpec((1, GMLP_DG), lambda i, j: (0, 0)),
            pl.BlockSpec((1, GMLP_DG), lambda i, j: (0, 0)),
        ],
        out_specs=[
            pl.BlockSpec((IN_TM, IN_TN), lambda i, j: (i, j)),
            pl.BlockSpec((IN_TM, GLA_GATE_RANK), lambda i, j: (i, 0)),
        ],
        out_shape=[
            jax.ShapeDtypeStruct((m, P_WIDTH), BF16),
            jax.ShapeDtypeStruct((m, GLA_GATE_RANK), F32),
        ],
        scratch_shapes=[pltpu.VMEM((IN_TM, D_MODEL), BF16)],
        compiler_params=pltpu.CompilerParams(
            dimension_semantics=("arbitrary", "arbitrary"),
            vmem_limit_bytes=VMEM_LIMIT),
    )(x2, norm_w, w_main, w_glr, b_gate, ln_w, ln_b)


def _gla_kernel(q_ref, k_ref, v_ref, rs_ref, glr_ref, wup_ref, ba_ref, gnw_ref,
                o_ref, state_ref, la_ref):
    t = pl.program_id(1)

    @pl.when(t == 0)
    def _():
        state_ref[...] = jnp.zeros_like(state_ref)

    z = jnp.dot(glr_ref[...].astype(BF16), wup_ref[...], preferred_element_type=F32)
    z = z + ba_ref[...]
    la_ref[...] = (jnp.minimum(z, 0.0) - jnp.log1p(jnp.exp(-jnp.abs(z)))) * (1.0 / GLA_TAU)

    row = lax.broadcasted_iota(jnp.int32, (CHUNK, CHUNK), 0)
    col = lax.broadcasted_iota(jnp.int32, (CHUNK, CHUNK), 1)
    tri = (row >= col).astype(BF16)

    def chunk_body(c, carry):
        rows = pl.ds(pl.multiple_of(c * CHUNK, CHUNK), CHUNK)
        la = la_ref[rows, :]
        la_hi = la.astype(BF16)
        la_lo = (la - la_hi.astype(F32)).astype(BF16)
        lcum = (jnp.dot(tri, la_hi, preferred_element_type=F32)
                + jnp.dot(tri, la_lo, preferred_element_type=F32))
        l_end = lcum[CHUNK - 1:CHUNK, :]
        k_dec = (k_ref[rows, :].astype(F32) * jnp.exp(l_end - lcum)).astype(BF16)
        decay = jnp.exp(l_end)
        q = q_ref[rows, :]
        v = v_ref[rows, :]
        rs = rs_ref[rows, :]
        for h in range(GLA_HEADS):
            kc = slice(h * GLA_DK, (h + 1) * GLA_DK)
            vc = slice(h * GLA_DV, (h + 1) * GLA_DV)
            kv_t = lax.dot_general(v[:, vc], k_dec[:, kc], (((0,), (0,)), ((), ())),
                                   preferred_element_type=F32)
            s = state_ref[h] * decay[:, kc] + kv_t
            state_ref[h] = s
            o = lax.dot_general(q[:, kc], s.astype(BF16), (((1,), (1,)), ((), ())),
                                preferred_element_type=F32)
            ms = jnp.mean(o * o, axis=-1, keepdims=True)
            o = o * lax.rsqrt(ms + EPS) * gnw_ref[...] * rs[:, vc].astype(F32)
            o_ref[rows, vc] = o.astype(BF16)
        return carry

    lax.fori_loop(0, GLA_TB // CHUNK, chunk_body, 0)


def _gla(p, glr, wup, b_alpha, gnw, batch, seq):
    m = p.shape[0]
    nt = seq // GLA_TB
    rowblk = lambda b, t: b * nt + t
    return pl.pallas_call(
        _gla_kernel,
        grid=(batch, nt),
        in_specs=[
            pl.BlockSpec((GLA_TB, QK_W), lambda b, t: (rowblk(b, t), COL_Q // QK_W)),
            pl.BlockSpec((GLA_TB, QK_W), lambda b, t: (rowblk(b, t), COL_K // QK_W)),
            pl.BlockSpec((GLA_TB, V_W), lambda b, t: (rowblk(b, t), COL_V // V_W)),
            pl.BlockSpec((GLA_TB, V_W), lambda b, t: (rowblk(b, t), COL_R // V_W)),
            pl.BlockSpec((GLA_TB, GLA_GATE_RANK), lambda b, t: (rowblk(b, t), 0)),
            pl.BlockSpec((GLA_GATE_RANK, QK_W), lambda b, t: (0, 0)),
            pl.BlockSpec((1, QK_W), lambda b, t: (0, 0)),
            pl.BlockSpec((1, GLA_DV), lambda b, t: (0, 0)),
        ],
        out_specs=pl.BlockSpec((GLA_TB, V_W), lambda b, t: (rowblk(b, t), 0)),
        out_shape=jax.ShapeDtypeStruct((m, V_W), BF16),
        scratch_shapes=[
            pltpu.VMEM((GLA_HEADS, GLA_DV, GLA_DK), F32),
            pltpu.VMEM((GLA_TB, QK_W), F32),
        ],
        compiler_params=pltpu.CompilerParams(
            dimension_semantics=("arbitrary", "arbitrary"),
            vmem_limit_bytes=VMEM_LIMIT),
    )(p, p, p, p, glr, wup, b_alpha, gnw)


def _merge_kernel(og_ref, gu_ref, gv_ref, gate0_ref, gate1_ref, x_ref, ws_ref, bs_ref,
                  wb0_ref, wb1_ref, wo_ref, h_ref, gm_ref):
    pos_t = lax.broadcasted_iota(jnp.int32, (GMLP_BLOCK, GMLP_BLOCK), 0) // CHUNK
    pos_s = lax.broadcasted_iota(jnp.int32, (GMLP_BLOCK, GMLP_BLOCK), 1) // CHUNK
    causal = pos_t >= pos_s
    for g in range(GMLP_GROUPS):
        w = jnp.where(causal, ws_ref[g], 0.0).astype(BF16)
        cols = slice(g * GMLP_DG, (g + 1) * GMLP_DG)
        for blk in range(MERGE_TM // GMLP_BLOCK):
            rows = pl.ds(blk * GMLP_BLOCK, GMLP_BLOCK)
            mixed = jnp.dot(w, gv_ref[rows, cols], preferred_element_type=F32) + bs_ref[g]
            gm_ref[rows, cols] = (gu_ref[rows, cols].astype(F32) * mixed).astype(BF16)

    d0 = jnp.dot(og_ref[...], wb0_ref[...], preferred_element_type=F32)
    d1 = jnp.dot(gm_ref[...], wb1_ref[...], preferred_element_type=F32)
    mixed = (gate0_ref[...].astype(F32) * d0 + gate1_ref[...].astype(F32) * d1).astype(BF16)
    h_ref[...] = x_ref[...] + jnp.dot(mixed, wo_ref[...], preferred_element_type=F32)


def _merge(o_gla, p, x2, w_spatial, b_sp, wb0, wb1, wo):
    m = x2.shape[0]
    const2 = lambda i: (0, 0)
    const3 = lambda i: (0, 0, 0)
    single = pl.Buffered(1)
    return pl.pallas_call(
        _merge_kernel,
        grid=(m // MERGE_TM,),
        in_specs=[
            pl.BlockSpec((MERGE_TM, V_W), lambda i: (i, 0)),
            pl.BlockSpec((MERGE_TM, D_MODEL), lambda i: (i, COL_GU // D_MODEL)),
            pl.BlockSpec((MERGE_TM, D_MODEL), lambda i: (i, COL_GV // D_MODEL)),
            pl.BlockSpec((MERGE_TM, D_MODEL), lambda i: (i, COL_GATE // D_MODEL)),
            pl.BlockSpec((MERGE_TM, D_MODEL), lambda i: (i, COL_GATE // D_MODEL + 1)),
            pl.BlockSpec((MERGE_TM, D_MODEL), lambda i: (i, 0)),
            pl.BlockSpec((GMLP_GROUPS, GMLP_BLOCK, GMLP_BLOCK), const3, pipeline_mode=single),
            pl.BlockSpec((GMLP_GROUPS, GMLP_BLOCK, GMLP_DG), const3, pipeline_mode=single),
            pl.BlockSpec((V_W, D_MODEL), const2, pipeline_mode=single),
            pl.BlockSpec((V_W, D_MODEL), const2, pipeline_mode=single),
            pl.BlockSpec((D_MODEL, D_MODEL), const2, pipeline_mode=single),
        ],
        out_specs=pl.BlockSpec((MERGE_TM, D_MODEL), lambda i: (i, 0)),
        out_shape=jax.ShapeDtypeStruct((m, D_MODEL), F32),
        scratch_shapes=[pltpu.VMEM((MERGE_TM, D_MODEL), BF16)],
        compiler_params=pltpu.CompilerParams(
            dimension_semantics=("arbitrary",),
            vmem_limit_bytes=VMEM_LIMIT),
    )(o_gla, p, p, p, p, x2, w_spatial, b_sp, wb0, wb1, wo)


def _ffn_kernel(h_ref, nw_ref, wu_ref, wd_ref, fw_ref, y_ref, hn_ref, acc_ref):
    f = pl.program_id(1)

    @pl.when(f == 0)
    def _():
        h = h_ref[...]
        ms = jnp.mean(h * h, axis=-1, keepdims=True)
        hn_ref[...] = (h * lax.rsqrt(ms + EPS) * nw_ref[...]).astype(BF16)

    up = jnp.dot(hn_ref[...], wu_ref[...], preferred_element_type=F32)
    up = jnp.square(jnp.maximum(up, 0.0)).astype(BF16)
    down = jnp.dot(up, wd_ref[...], preferred_element_type=F32)

    @pl.when(f == 0)
    def _():
        acc_ref[...] = down

    @pl.when(f > 0)
    def _():
        acc_ref[...] += down

    @pl.when(f == pl.num_programs(1) - 1)
    def _():
        h2 = h_ref[...] + acc_ref[...]
        ms = jnp.mean(h2 * h2, axis=-1, keepdims=True)
        y_ref[...] = h2 * lax.rsqrt(ms + EPS) * fw_ref[...]


def _ffn(h, norm_w, wu, wd, final_w):
    m = h.shape[0]
    return pl.pallas_call(
        _ffn_kernel,
        grid=(m // FFN_TM, D_FF // FFN_TF),
        in_specs=[
            pl.BlockSpec((FFN_TM, D_MODEL), lambda i, f: (i, 0)),
            pl.BlockSpec((1, D_MODEL), lambda i, f: (0, 0)),
            pl.BlockSpec((D_MODEL, FFN_TF), lambda i, f: (0, f)),
            pl.BlockSpec((FFN_TF, D_MODEL), lambda i, f: (f, 0)),
            pl.BlockSpec((1, D_MODEL), lambda i, f: (0, 0)),
        ],
        out_specs=pl.BlockSpec((FFN_TM, D_MODEL), lambda i, f: (i, 0)),
        out_shape=jax.ShapeDtypeStruct((m, D_MODEL), F32),
        scratch_shapes=[
            pltpu.VMEM((FFN_TM, D_MODEL), BF16),
            pltpu.VMEM((FFN_TM, D_MODEL), F32),
        ],
        compiler_params=pltpu.CompilerParams(
            dimension_semantics=("arbitrary", "arbitrary"),
            vmem_limit_bytes=VMEM_LIMIT),
    )(h, norm_w, wu, wd, final_w)


def _layer(h2, batch, seq, norm_mix_w, w_in, w_alpha_up, b_alpha, gla_norm_w, gmlp_ln_w,
           gmlp_ln_b, w_spatial, b_spatial, b_gate, w_branch, w_out, norm_mlp_w, w_ff_up,
           w_ff_down, final_w):
    glr0 = COL_R + V_W
    glr1 = glr0 + GLA_GATE_RANK
    w_main = jnp.concatenate([w_in[:, :glr0], w_in[:, glr1:]], axis=1).astype(BF16)
    w_glr = w_in[:, glr0:glr1].astype(BF16)
    p, glr = _in_proj(h2, norm_mix_w.reshape(1, D_MODEL), w_main, w_glr,
                      b_gate.reshape(1, 2 * D_MODEL), gmlp_ln_w.reshape(1, GMLP_DG),
                      gmlp_ln_b.reshape(1, GMLP_DG))
    o_gla = _gla(p, glr, w_alpha_up.astype(BF16), b_alpha.reshape(1, QK_W),
                 gla_norm_w.reshape(1, GLA_DV), batch, seq)
    b_sp = jnp.broadcast_to(b_spatial[:, :, None], (GMLP_GROUPS, GMLP_BLOCK, GMLP_DG))
    h_mid = _merge(o_gla, p, h2, w_spatial, b_sp, w_branch[0].astype(BF16),
                   w_branch[1].astype(BF16), w_out.astype(BF16))
    return _ffn(h_mid, norm_mlp_w.reshape(1, D_MODEL), w_ff_up.astype(BF16),
                w_ff_down.astype(BF16), final_w)


def kernel(x, norm_mix_w, w_in, w_alpha_up, b_alpha, gla_norm_w, gmlp_ln_w, gmlp_ln_b,
           w_spatial, b_spatial, b_gate, w_branch, w_out, norm_mlp_w, w_ff_up, w_ff_down,
           norm_final_w):
    batch, seq, d = x.shape
    depth = w_in.shape[0]
    assert d == D_MODEL and depth == 1 and seq % GLA_TB == 0
    h2 = x.reshape(batch * seq, d)
    y = _layer(h2, batch, seq, norm_mix_w[0], w_in[0], w_alpha_up[0], b_alpha[0],
               gla_norm_w[0], gmlp_ln_w[0], gmlp_ln_b[0], w_spatial[0], b_spatial[0],
               b_gate[0], w_branch[0], w_out[0], norm_mlp_w[0], w_ff_up[0], w_ff_down[0],
               norm_final_w.reshape(1, D_MODEL))
    return y.reshape(batch, seq, d)
```

```python
import jax
import jax.numpy as jnp
from jax import lax
from jax.experimental import pallas as pl
from jax.experimental.pallas import tpu as pltpu

D_MODEL = 2048
CHUNK = 64
GLA_HEADS = 4
GLA_DK = D_MODEL // 8
GLA_DV = D_MODEL // 4
GLA_GATE_RANK = 16
GLA_TAU = 16.0
GMLP_BLOCK = 128
GMLP_GROUPS = 8
GMLP_DG = D_MODEL // GMLP_GROUPS
D_FF = 4 * D_MODEL
EPS = 1e-6
QK_W = GLA_HEADS * GLA_DK
V_W = GLA_HEADS * GLA_DV
D_IN = 2 * QK_W + 2 * V_W + GLA_GATE_RANK + 4 * D_MODEL

F32 = jnp.float32
BF16 = jnp.bfloat16

LANES = 128

COL_Q = 0
COL_K = QK_W
COL_V = 2 * QK_W
COL_R = COL_V + V_W
COL_GU = COL_R + V_W
COL_GV = COL_GU + D_MODEL
COL_GATE = COL_GV + D_MODEL
P_WIDTH = COL_GATE + 2 * D_MODEL

V7X_VMEM_BYTES = 64 * 1024 * 1024
VMEM_LIMIT = V7X_VMEM_BYTES - 8 * 1024 * 1024

IN_TM = 1024
IN_TN = 1024
IN_RC = 256
CAST_RC = 256
GLA_TB = 512
MERGE_TM = 256
FFN_TM = 512
FFN_TF = 1024


def _sigmoid(x):
    return 0.5 * (1.0 + jnp.tanh(0.5 * x))


def _gelu(x):
    return jax.nn.gelu(x, approximate=True)


def _dot_nt(a, b_t):
    return lax.dot_general(a, b_t, (((1,), (1,)), ((), ())), preferred_element_type=F32)


def _qnorm_kernel(x_ref, nw_ref, wq_ref, wg_ref, xn_ref, q_ref, glr_ref, wqb_ref, wgb_ref):
    @pl.when(pl.program_id(0) == 0)
    def _():
        wqb_ref[...] = wq_ref[...].astype(BF16)
        wgb_ref[...] = wg_ref[...].astype(BF16)

    for c in range(IN_TM // IN_RC):
        rows = pl.ds(c * IN_RC, IN_RC)
        x = x_ref[rows, :]
        ms = jnp.mean(x * x, axis=-1, keepdims=True)
        xn = (x * lax.rsqrt(ms + EPS) * nw_ref[...]).astype(BF16)
        xn_ref[rows, :] = xn
        q_ref[rows, :] = (_dot_nt(xn, wqb_ref[...]) * (GLA_DK ** -0.5)).astype(BF16)
        glr_ref[rows, :] = _dot_nt(xn, wgb_ref[...])


def _qnorm(x2, norm_w, w_in_t):
    m = x2.shape[0]
    single = pl.Buffered(1)
    return pl.pallas_call(
        _qnorm_kernel,
        grid=(m // IN_TM,),
        in_specs=[
            pl.BlockSpec((IN_TM, D_MODEL), lambda i: (i, 0)),
            pl.BlockSpec((1, D_MODEL), lambda i: (0, 0)),
            pl.BlockSpec((QK_W, D_MODEL), lambda i: (COL_Q // QK_W, 0), pipeline_mode=single),
            pl.BlockSpec((LANES, D_MODEL), lambda i: (COL_GU // LANES, 0), pipeline_mode=single),
        ],
        out_specs=[
            pl.BlockSpec((IN_TM, D_MODEL), lambda i: (i, 0)),
            pl.BlockSpec((IN_TM, QK_W), lambda i: (i, 0)),
            pl.BlockSpec((IN_TM, LANES), lambda i: (i, 0)),
        ],
        out_shape=[
            jax.ShapeDtypeStruct((m, D_MODEL), BF16),
            jax.ShapeDtypeStruct((m, QK_W), BF16),
            jax.ShapeDtypeStruct((m, LANES), F32),
        ],
        scratch_shapes=[pltpu.VMEM((QK_W, D_MODEL), BF16), pltpu.VMEM((LANES, D_MODEL), BF16)],
        compiler_params=pltpu.CompilerParams(
            dimension_semantics=("arbitrary",), vmem_limit_bytes=VMEM_LIMIT),
        name="qnorm",
    )(x2, norm_w, w_in_t, w_in_t)


_J_K, _J_R, _J_GU, _J_GV, _J_GATE = (c // IN_TN for c in (COL_K, COL_R, COL_GU, COL_GV, COL_GATE))
_N_J = P_WIDTH // IN_TN


def _in_proj_kernel(xn_ref, w_ref, bg_ref, lnw_ref, lnb_ref, p_ref, wb_ref):
    j = pl.program_id(0) + _J_K
    i = pl.program_id(1)

    @pl.when(i == 0)
    def _():
        for rc in range(IN_TN // CAST_RC):
            rows = pl.ds(rc * CAST_RC, CAST_RC)
            wb_ref[rows, :] = w_ref[rows, :].astype(BF16)

    def run(epilogue):
        for c in range(IN_TM // IN_RC):
            rows = pl.ds(c * IN_RC, IN_RC)
            acc = _dot_nt(xn_ref[rows, :], wb_ref[...])
            p_ref[rows, :] = epilogue(acc).astype(BF16)

    def epi_ln(acc):
        outs = []
        for g in range(IN_TN // GMLP_DG):
            cols = slice(g * GMLP_DG, (g + 1) * GMLP_DG)
            v = _gelu(acc[:, cols])
            mu = jnp.mean(v, axis=-1, keepdims=True)
            vc = v - mu
            var = jnp.mean(vc * vc, axis=-1, keepdims=True)
            outs.append(vc * lax.rsqrt(var + EPS) * lnw_ref[...] + lnb_ref[...])
        return jnp.concatenate(outs, axis=-1)

    @pl.when(j < _J_R)
    def _():
        run(lambda a: a)

    @pl.when(jnp.logical_and(j >= _J_R, j < _J_GU))
    def _():
        run(lambda a: a * _sigmoid(a))

    @pl.when(jnp.logical_and(j >= _J_GU, j < _J_GV))
    def _():
        run(_gelu)

    @pl.when(jnp.logical_and(j >= _J_GV, j < _J_GATE))
    def _():
        run(epi_ln)

    @pl.when(j >= _J_GATE)
    def _():
        run(lambda a: _sigmoid(a + bg_ref[...]))


def _in_proj(xn, w_in_t, b_gate, ln_w, ln_b):
    m = xn.shape[0]
    n_gate_blocks = (2 * D_MODEL) // IN_TN

    def w_rows(jj, i):
        j = jj + _J_K
        skip = jnp.where(j >= _J_GU, GLA_GATE_RANK, 0)
        return (pl.multiple_of(j * IN_TN + skip, GLA_GATE_RANK), 0)

    return pl.pallas_call(
        _in_proj_kernel,
        grid=(_N_J - _J_K, m // IN_TM),
        in_specs=[
            pl.BlockSpec((IN_TM, D_MODEL), lambda jj, i: (i, 0)),
            pl.BlockSpec((pl.Element(IN_TN), pl.Element(D_MODEL)), w_rows),
            pl.BlockSpec((1, IN_TN),
                         lambda jj, i: (0, jnp.clip(jj + _J_K - _J_GATE, 0, n_gate_blocks - 1))),
            pl.BlockSpec((1, GMLP_DG), lambda jj, i: (0, 0)),
            pl.BlockSpec((1, GMLP_DG), lambda jj, i: (0, 0)),
        ],
        out_specs=pl.BlockSpec((IN_TM, IN_TN), lambda jj, i: (i, jj + _J_K)),
        out_shape=jax.ShapeDtypeStruct((m, P_WIDTH), BF16),
        scratch_shapes=[pltpu.VMEM((IN_TN, D_MODEL), BF16)],
        compiler_params=pltpu.CompilerParams(
            dimension_semantics=("arbitrary", "arbitrary"), vmem_limit_bytes=VMEM_LIMIT),
        name="in_proj",
    )(xn, w_in_t, b_gate, ln_w, ln_b)


def _gla_kernel(q_ref, k_ref, v_ref, rs_ref, glr_ref, wup_ref, ba_ref, gnw_ref,
                o_ref, state_ref, la_ref):
    t = pl.program_id(1)

    @pl.when(t == 0)
    def _():
        state_ref[...] = jnp.zeros_like(state_ref)

    glr = glr_ref[:, :GLA_GATE_RANK].astype(BF16)
    z = jnp.dot(glr, wup_ref[...], preferred_element_type=F32) + ba_ref[...]
    la_ref[...] = (jnp.minimum(z, 0.0) - jnp.log1p(jnp.exp(-jnp.abs(z)))) * (1.0 / GLA_TAU)

    row = lax.broadcasted_iota(jnp.int32, (CHUNK, CHUNK), 0)
    col = lax.broadcasted_iota(jnp.int32, (CHUNK, CHUNK), 1)
    tri = (row >= col).astype(BF16)

    def chunk_body(c, carry):
        rows = pl.ds(pl.multiple_of(c * CHUNK, CHUNK), CHUNK)
        la = la_ref[rows, :]
        la_hi = la.astype(BF16)
        la_lo = (la - la_hi.astype(F32)).astype(BF16)
        lcum = (jnp.dot(tri, la_hi, preferred_element_type=F32)
                + jnp.dot(tri, la_lo, preferred_element_type=F32))
        l_end = lcum[CHUNK - 1:CHUNK, :]
        k_dec = (k_ref[rows, :].astype(F32) * jnp.exp(l_end - lcum)).astype(BF16)
        decay = jnp.exp(l_end)
        q = q_ref[rows, :]
        v = v_ref[rows, :]
        rs = rs_ref[rows, :]
        for h in range(GLA_HEADS):
            kc = slice(h * GLA_DK, (h + 1) * GLA_DK)
            vc = slice(h * GLA_DV, (h + 1) * GLA_DV)
            kv_t = lax.dot_general(v[:, vc], k_dec[:, kc], (((0,), (0,)), ((), ())),
                                   preferred_element_type=F32)
            s = state_ref[h] * decay[:, kc] + kv_t
            state_ref[h] = s
            o = lax.dot_general(q[:, kc], s.astype(BF16), (((1,), (1,)), ((), ())),
                                preferred_element_type=F32)
            ms = jnp.mean(o * o, axis=-1, keepdims=True)
            o = o * lax.rsqrt(ms + EPS) * gnw_ref[...] * rs[:, vc].astype(F32)
            o_ref[rows, vc] = o.astype(BF16)
        return carry

    lax.fori_loop(0, GLA_TB // CHUNK, chunk_body, 0)


def _gla(q, p, glr, wup, b_alpha, gnw, batch, seq):
    m = p.shape[0]
    nt = seq // GLA_TB
    rowblk = lambda b, t: b * nt + t
    return pl.pallas_call(
        _gla_kernel,
        grid=(batch, nt),
        in_specs=[
            pl.BlockSpec((GLA_TB, QK_W), lambda b, t: (rowblk(b, t), 0)),
            pl.BlockSpec((GLA_TB, QK_W), lambda b, t: (rowblk(b, t), COL_K // QK_W)),
            pl.BlockSpec((GLA_TB, V_W), lambda b, t: (rowblk(b, t), COL_V // V_W)),
            pl.BlockSpec((GLA_TB, V_W), lambda b, t: (rowblk(b, t), COL_R // V_W)),
            pl.BlockSpec((GLA_TB, LANES), lambda b, t: (rowblk(b, t), 0)),
            pl.BlockSpec((GLA_GATE_RANK, QK_W), lambda b, t: (0, 0)),
            pl.BlockSpec((1, QK_W), lambda b, t: (0, 0)),
            pl.BlockSpec((1, GLA_DV), lambda b, t: (0, 0)),
        ],
        out_specs=pl.BlockSpec((GLA_TB, V_W), lambda b, t: (rowblk(b, t), 0)),
        out_shape=jax.ShapeDtypeStruct((m, V_W), BF16),
        scratch_shapes=[
            pltpu.VMEM((GLA_HEADS, GLA_DV, GLA_DK), F32),
            pltpu.VMEM((GLA_TB, QK_W), F32),
        ],
        compiler_params=pltpu.CompilerParams(
            dimension_semantics=("arbitrary", "arbitrary"), vmem_limit_bytes=VMEM_LIMIT),
        name="gla",
    )(q, p, p, p, glr, wup, b_alpha, gnw)


def _merge_kernel(og_ref, gu_ref, gv_ref, gate0_ref, gate1_ref, x_ref, ws_ref, bs_ref,
                  wb0_ref, wb1_ref, wo_ref, h_ref, gm_ref):
    pos_t = lax.broadcasted_iota(jnp.int32, (GMLP_BLOCK, GMLP_BLOCK), 0) // CHUNK
    pos_s = lax.broadcasted_iota(jnp.int32, (GMLP_BLOCK, GMLP_BLOCK), 1) // CHUNK
    causal = pos_t >= pos_s
    for g in range(GMLP_GROUPS):
        w = jnp.where(causal, ws_ref[g], 0.0).astype(BF16)
        cols = slice(g * GMLP_DG, (g + 1) * GMLP_DG)
        for blk in range(MERGE_TM // GMLP_BLOCK):
            rows = pl.ds(blk * GMLP_BLOCK, GMLP_BLOCK)
            mixed = jnp.dot(w, gv_ref[rows, cols], preferred_element_type=F32) + bs_ref[g]
            gm_ref[rows, cols] = (gu_ref[rows, cols].astype(F32) * mixed).astype(BF16)

    d0 = jnp.dot(og_ref[...], wb0_ref[...], preferred_element_type=F32)
    d1 = jnp.dot(gm_ref[...], wb1_ref[...], preferred_element_type=F32)
    mixed = (gate0_ref[...].astype(F32) * d0 + gate1_ref[...].astype(F32) * d1).astype(BF16)
    h_ref[...] = x_ref[...] + jnp.dot(mixed, wo_ref[...], preferred_element_type=F32)


def _merge(o_gla, p, x2, w_spatial, b_sp, wb0, wb1, wo):
    m = x2.shape[0]
    const2 = lambda i: (0, 0)
    const3 = lambda i: (0, 0, 0)
    single = pl.Buffered(1)
    return pl.pallas_call(
        _merge_kernel,
        grid=(m // MERGE_TM,),
        in_specs=[
            pl.BlockSpec((MERGE_TM, V_W), lambda i: (i, 0)),
            pl.BlockSpec((MERGE_TM, D_MODEL), lambda i: (i, COL_GU // D_MODEL)),
            pl.BlockSpec((MERGE_TM, D_MODEL), lambda i: (i, COL_GV // D_MODEL)),
            pl.BlockSpec((MERGE_TM, D_MODEL), lambda i: (i, COL_GATE // D_MODEL)),
            pl.BlockSpec((MERGE_TM, D_MODEL), lambda i: (i, COL_GATE // D_MODEL + 1)),
            pl.BlockSpec((MERGE_TM, D_MODEL), lambda i: (i, 0)),
            pl.BlockSpec((GMLP_GROUPS, GMLP_BLOCK, GMLP_BLOCK), const3, pipeline_mode=single),
            pl.BlockSpec((GMLP_GROUPS, GMLP_BLOCK, GMLP_DG), const3, pipeline_mode=single),
            pl.BlockSpec((V_W, D_MODEL), const2, pipeline_mode=single),
            pl.BlockSpec((V_W, D_MODEL), const2, pipeline_mode=single),
            pl.BlockSpec((D_MODEL, D_MODEL), const2, pipeline_mode=single),
        ],
        out_specs=pl.BlockSpec((MERGE_TM, D_MODEL), lambda i: (i, 0)),
        out_shape=jax.ShapeDtypeStruct((m, D_MODEL), F32),
        scratch_shapes=[pltpu.VMEM((MERGE_TM, D_MODEL), BF16)],
        compiler_params=pltpu.CompilerParams(
            dimension_semantics=("arbitrary",), vmem_limit_bytes=VMEM_LIMIT),
        name="merge",
    )(o_gla, p, p, p, p, x2, w_spatial, b_sp, wb0, wb1, wo)


def _ffn_kernel(h_ref, nw_ref, wu_ref, wd_ref, fw_ref, y_ref, hn_ref, acc_ref):
    f = pl.program_id(1)

    @pl.when(f == 0)
    def _():
        h = h_ref[...]
        ms = jnp.mean(h * h, axis=-1, keepdims=True)
        hn_ref[...] = (h * lax.rsqrt(ms + EPS) * nw_ref[...]).astype(BF16)
        acc_ref[...] = h

    up = jnp.dot(hn_ref[...], wu_ref[...], preferred_element_type=F32)
    up = jnp.square(jnp.maximum(up, 0.0)).astype(BF16)
    acc_ref[...] += jnp.dot(up, wd_ref[...], preferred_element_type=F32)

    @pl.when(f == pl.num_programs(1) - 1)
    def _():
        h2 = acc_ref[...]
        ms = jnp.mean(h2 * h2, axis=-1, keepdims=True)
        y_ref[...] = h2 * lax.rsqrt(ms + EPS) * fw_ref[...]


def _ffn(h, norm_w, wu, wd, final_w):
    m = h.shape[0]
    return pl.pallas_call(
        _ffn_kernel,
        grid=(m // FFN_TM, D_FF // FFN_TF),
        in_specs=[
            pl.BlockSpec((FFN_TM, D_MODEL), lambda i, f: (i, 0)),
            pl.BlockSpec((1, D_MODEL), lambda i, f: (0, 0)),
            pl.BlockSpec((D_MODEL, FFN_TF), lambda i, f: (0, f)),
            pl.BlockSpec((FFN_TF, D_MODEL), lambda i, f: (f, 0)),
            pl.BlockSpec((1, D_MODEL), lambda i, f: (0, 0)),
        ],
        out_specs=pl.BlockSpec((FFN_TM, D_MODEL), lambda i, f: (i, 0)),
        out_shape=jax.ShapeDtypeStruct((m, D_MODEL), F32),
        scratch_shapes=[
            pltpu.VMEM((FFN_TM, D_MODEL), BF16),
            pltpu.VMEM((FFN_TM, D_MODEL), F32),
        ],
        compiler_params=pltpu.CompilerParams(
            dimension_semantics=("arbitrary", "arbitrary"), vmem_limit_bytes=VMEM_LIMIT),
        name="ffn",
    )(h, norm_w, wu, wd, final_w)


def _layer(h2, batch, seq, norm_mix_w, w_in, w_alpha_up, b_alpha, gla_norm_w, gmlp_ln_w,
           gmlp_ln_b, w_spatial, b_spatial, b_gate, w_branch, w_out, norm_mlp_w, w_ff_up,
           w_ff_down, final_w):
    w_in_t = w_in.T
    xn, q, glr = _qnorm(h2, norm_mix_w.reshape(1, D_MODEL), w_in_t)
    p = _in_proj(xn, w_in_t, b_gate.reshape(1, 2 * D_MODEL), gmlp_ln_w.reshape(1, GMLP_DG),
                 gmlp_ln_b.reshape(1, GMLP_DG))
    o_gla = _gla(q, p, glr, w_alpha_up.astype(BF16), b_alpha.reshape(1, QK_W),
                 gla_norm_w.reshape(1, GLA_DV), batch, seq)
    b_sp = jnp.broadcast_to(b_spatial[:, :, None], (GMLP_GROUPS, GMLP_BLOCK, GMLP_DG))
    h_mid = _merge(o_gla, p, h2, w_spatial, b_sp, w_branch[0].astype(BF16),
                   w_branch[1].astype(BF16), w_out.astype(BF16))
    return _ffn(h_mid, norm_mlp_w.reshape(1, D_MODEL), w_ff_up.astype(BF16),
                w_ff_down.astype(BF16), final_w)


def kernel(x, norm_mix_w, w_in, w_alpha_up, b_alpha, gla_norm_w, gmlp_ln_w, gmlp_ln_b,
           w_spatial, b_spatial, b_gate, w_branch, w_out, norm_mlp_w, w_ff_up, w_ff_down,
           norm_final_w):
    batch, seq, d = x.shape
    depth = w_in.shape[0]
    assert d == D_MODEL and depth == 1 and seq % GLA_TB == 0
    assert w_in.shape[2] == D_IN
    h2 = x.reshape(batch * seq, d)
    y = _layer(h2, batch, seq, norm_mix_w[0], w_in[0], w_alpha_up[0], b_alpha[0],
               gla_norm_w[0], gmlp_ln_w[0], gmlp_ln_b[0], w_spatial[0], b_spatial[0],
               b_gate[0], w_branch[0], w_out[0], norm_mlp_w[0], w_ff_up[0], w_ff_down[0],
               norm_final_w.reshape(1, D_MODEL))
    return y.reshape(batch, seq, d)
```

```python
import jax
import jax.numpy as jnp
from jax import lax
from jax.experimental import pallas as pl
from jax.experimental.pallas import tpu as pltpu

D_MODEL = 2048
CHUNK = 64
GLA_HEADS = 4
GLA_DK = D_MODEL // 8
GLA_DV = D_MODEL // 4
GLA_GATE_RANK = 16
GLA_TAU = 16.0
GMLP_BLOCK = 128
GMLP_GROUPS = 8
GMLP_DG = D_MODEL // GMLP_GROUPS
D_FF = 4 * D_MODEL
EPS = 1e-6
QK_W = GLA_HEADS * GLA_DK
V_W = GLA_HEADS * GLA_DV
D_IN = 2 * QK_W + 2 * V_W + GLA_GATE_RANK + 4 * D_MODEL

F32 = jnp.float32
BF16 = jnp.bfloat16

LANES = 128

COL_Q = 0
COL_K = QK_W
COL_V = 2 * QK_W
COL_R = COL_V + V_W
COL_GU = COL_R + V_W
COL_GV = COL_GU + D_MODEL
COL_GATE = COL_GV + D_MODEL
P_WIDTH = COL_GATE + 2 * D_MODEL

V7X_VMEM_BYTES = 64 * 1024 * 1024
VMEM_LIMIT = V7X_VMEM_BYTES - 8 * 1024 * 1024

IN_TM = 1024
IN_TN = 1024
IN_RC = 256
CAST_RC = 256
GLA_TB = 512
MERGE_TM = 256
FFN_TM = 512
FFN_TF = 1024


def _sigmoid(x):
    return 0.5 * (1.0 + jnp.tanh(0.5 * x))


def _gelu(x):
    return jax.nn.gelu(x, approximate=True)


def _cast_transposed(wt_ref, wb_ref, n_rows):
    for rc in range(n_rows // CAST_RC):
        rows = pl.ds(rc * CAST_RC, CAST_RC)
        wb_ref[:, rc * CAST_RC:(rc + 1) * CAST_RC] = wt_ref[rows, :].T.astype(BF16)


def _qnorm_kernel(x_ref, nw_ref, wq_ref, wg_ref, xn_ref, q_ref, glr_ref, wqb_ref, wgb_ref):
    @pl.when(pl.program_id(0) == 0)
    def _():
        _cast_transposed(wq_ref, wqb_ref, QK_W)
        wgb_ref[...] = wg_ref[...].T.astype(BF16)

    for c in range(IN_TM // IN_RC):
        rows = pl.ds(c * IN_RC, IN_RC)
        x = x_ref[rows, :]
        ms = jnp.mean(x * x, axis=-1, keepdims=True)
        xn = (x * lax.rsqrt(ms + EPS) * nw_ref[...]).astype(BF16)
        xn_ref[rows, :] = xn
        q = jnp.dot(xn, wqb_ref[...], preferred_element_type=F32)
        q_ref[rows, :] = (q * (GLA_DK ** -0.5)).astype(BF16)
        glr_ref[rows, :] = jnp.dot(xn, wgb_ref[...], preferred_element_type=F32)


def _qnorm(x2, norm_w, w_in_t):
    m = x2.shape[0]
    single = pl.Buffered(1)
    return pl.pallas_call(
        _qnorm_kernel,
        grid=(m // IN_TM,),
        in_specs=[
            pl.BlockSpec((IN_TM, D_MODEL), lambda i: (i, 0)),
            pl.BlockSpec((1, D_MODEL), lambda i: (0, 0)),
            pl.BlockSpec((QK_W, D_MODEL), lambda i: (COL_Q // QK_W, 0), pipeline_mode=single),
            pl.BlockSpec((LANES, D_MODEL), lambda i: (COL_GU // LANES, 0), pipeline_mode=single),
        ],
        out_specs=[
            pl.BlockSpec((IN_TM, D_MODEL), lambda i: (i, 0)),
            pl.BlockSpec((IN_TM, QK_W), lambda i: (i, 0)),
            pl.BlockSpec((IN_TM, LANES), lambda i: (i, 0)),
        ],
        out_shape=[
            jax.ShapeDtypeStruct((m, D_MODEL), BF16),
            jax.ShapeDtypeStruct((m, QK_W), BF16),
            jax.ShapeDtypeStruct((m, LANES), F32),
        ],
        scratch_shapes=[pltpu.VMEM((D_MODEL, QK_W), BF16), pltpu.VMEM((D_MODEL, LANES), BF16)],
        compiler_params=pltpu.CompilerParams(
            dimension_semantics=("arbitrary",), vmem_limit_bytes=VMEM_LIMIT),
        name="qnorm",
    )(x2, norm_w, w_in_t, w_in_t)


_J_K, _J_R, _J_GU, _J_GV, _J_GATE = (c // IN_TN for c in (COL_K, COL_R, COL_GU, COL_GV, COL_GATE))
_N_J = P_WIDTH // IN_TN


def _in_proj_kernel(xn_ref, w_ref, bg_ref, lnw_ref, lnb_ref, p_ref, wb_ref):
    j = pl.program_id(0) + _J_K
    i = pl.program_id(1)

    @pl.when(i == 0)
    def _():
        _cast_transposed(w_ref, wb_ref, IN_TN)

    def run(epilogue):
        for c in range(IN_TM // IN_RC):
            rows = pl.ds(c * IN_RC, IN_RC)
            acc = jnp.dot(xn_ref[rows, :], wb_ref[...], preferred_element_type=F32)
            p_ref[rows, :] = epilogue(acc).astype(BF16)

    def epi_ln(acc):
        outs = []
        for g in range(IN_TN // GMLP_DG):
            cols = slice(g * GMLP_DG, (g + 1) * GMLP_DG)
            v = _gelu(acc[:, cols])
            mu = jnp.mean(v, axis=-1, keepdims=True)
            vc = v - mu
            var = jnp.mean(vc * vc, axis=-1, keepdims=True)
            outs.append(vc * lax.rsqrt(var + EPS) * lnw_ref[...] + lnb_ref[...])
        return jnp.concatenate(outs, axis=-1)

    @pl.when(j < _J_R)
    def _():
        run(lambda a: a)

    @pl.when(jnp.logical_and(j >= _J_R, j < _J_GU))
    def _():
        run(lambda a: a * _sigmoid(a))

    @pl.when(jnp.logical_and(j >= _J_GU, j < _J_GV))
    def _():
        run(_gelu)

    @pl.when(jnp.logical_and(j >= _J_GV, j < _J_GATE))
    def _():
        run(epi_ln)

    @pl.when(j >= _J_GATE)
    def _():
        run(lambda a: _sigmoid(a + bg_ref[...]))


def _in_proj(xn, w_in_t, b_gate, ln_w, ln_b):
    m = xn.shape[0]
    n_gate_blocks = (2 * D_MODEL) // IN_TN

    def w_rows(jj, i):
        j = jj + _J_K
        skip = jnp.where(j >= _J_GU, GLA_GATE_RANK, 0)
        return (pl.multiple_of(j * IN_TN + skip, GLA_GATE_RANK), 0)

    return pl.pallas_call(
        _in_proj_kernel,
        grid=(_N_J - _J_K, m // IN_TM),
        in_specs=[
            pl.BlockSpec((IN_TM, D_MODEL), lambda jj, i: (i, 0)),
            pl.BlockSpec((pl.Element(IN_TN), pl.Element(D_MODEL)), w_rows),
            pl.BlockSpec((1, IN_TN),
                         lambda jj, i: (0, jnp.clip(jj + _J_K - _J_GATE, 0, n_gate_blocks - 1))),
            pl.BlockSpec((1, GMLP_DG), lambda jj, i: (0, 0)),
            pl.BlockSpec((1, GMLP_DG), lambda jj, i: (0, 0)),
        ],
        out_specs=pl.BlockSpec((IN_TM, IN_TN), lambda jj, i: (i, jj + _J_K)),
        out_shape=jax.ShapeDtypeStruct((m, P_WIDTH), BF16),
        scratch_shapes=[pltpu.VMEM((D_MODEL, IN_TN), BF16)],
        compiler_params=pltpu.CompilerParams(
            dimension_semantics=("arbitrary", "arbitrary"), vmem_limit_bytes=VMEM_LIMIT),
        name="in_proj",
    )(xn, w_in_t, b_gate, ln_w, ln_b)


def _gla_kernel(q_ref, k_ref, v_ref, rs_ref, glr_ref, wup_ref, ba_ref, gnw_ref,
                o_ref, state_ref, la_ref):
    t = pl.program_id(1)

    @pl.when(t == 0)
    def _():
        state_ref[...] = jnp.zeros_like(state_ref)

    glr = glr_ref[:, :GLA_GATE_RANK].astype(BF16)
    z = jnp.dot(glr, wup_ref[...], preferred_element_type=F32) + ba_ref[...]
    la_ref[...] = (jnp.minimum(z, 0.0) - jnp.log1p(jnp.exp(-jnp.abs(z)))) * (1.0 / GLA_TAU)

    row = lax.broadcasted_iota(jnp.int32, (CHUNK, CHUNK), 0)
    col = lax.broadcasted_iota(jnp.int32, (CHUNK, CHUNK), 1)
    tri = (row >= col).astype(BF16)

    def chunk_body(c, carry):
        rows = pl.ds(pl.multiple_of(c * CHUNK, CHUNK), CHUNK)
        la = la_ref[rows, :]
        la_hi = la.astype(BF16)
        la_lo = (la - la_hi.astype(F32)).astype(BF16)
        lcum = (jnp.dot(tri, la_hi, preferred_element_type=F32)
                + jnp.dot(tri, la_lo, preferred_element_type=F32))
        l_end = lcum[CHUNK - 1:CHUNK, :]
        k_dec = (k_ref[rows, :].astype(F32) * jnp.exp(l_end - lcum)).astype(BF16)
        decay = jnp.exp(l_end)
        q = q_ref[rows, :]
        v = v_ref[rows, :]
        rs = rs_ref[rows, :]
        for h in range(GLA_HEADS):
            kc = slice(h * GLA_DK, (h + 1) * GLA_DK)
            vc = slice(h * GLA_DV, (h + 1) * GLA_DV)
            kv_t = lax.dot_general(v[:, vc], k_dec[:, kc], (((0,), (0,)), ((), ())),
                                   preferred_element_type=F32)
            s = state_ref[h] * decay[:, kc] + kv_t
            state_ref[h] = s
            o = lax.dot_general(q[:, kc], s.astype(BF16), (((1,), (1,)), ((), ())),
                                preferred_element_type=F32)
            ms = jnp.mean(o * o, axis=-1, keepdims=True)
            o = o * lax.rsqrt(ms + EPS) * gnw_ref[...] * rs[:, vc].astype(F32)
            o_ref[rows, vc] = o.astype(BF16)
        return carry

    lax.fori_loop(0, GLA_TB // CHUNK, chunk_body, 0)


def _gla(q, p, glr, wup, b_alpha, gnw, batch, seq):
    m = p.shape[0]
    nt = seq // GLA_TB
    rowblk = lambda b, t: b * nt + t
    return pl.pallas_call(
        _gla_kernel,
        grid=(batch, nt),
        in_specs=[
            pl.BlockSpec((GLA_TB, QK_W), lambda b, t: (rowblk(b, t), 0)),
            pl.BlockSpec((GLA_TB, QK_W), lambda b, t: (rowblk(b, t), COL_K // QK_W)),
            pl.BlockSpec((GLA_TB, V_W), lambda b, t: (rowblk(b, t), COL_V // V_W)),
            pl.BlockSpec((GLA_TB, V_W), lambda b, t: (rowblk(b, t), COL_R // V_W)),
            pl.BlockSpec((GLA_TB, LANES), lambda b, t: (rowblk(b, t), 0)),
            pl.BlockSpec((GLA_GATE_RANK, QK_W), lambda b, t: (0, 0)),
            pl.BlockSpec((1, QK_W), lambda b, t: (0, 0)),
            pl.BlockSpec((1, GLA_DV), lambda b, t: (0, 0)),
        ],
        out_specs=pl.BlockSpec((GLA_TB, V_W), lambda b, t: (rowblk(b, t), 0)),
        out_shape=jax.ShapeDtypeStruct((m, V_W), BF16),
        scratch_shapes=[
            pltpu.VMEM((GLA_HEADS, GLA_DV, GLA_DK), F32),
            pltpu.VMEM((GLA_TB, QK_W), F32),
        ],
        compiler_params=pltpu.CompilerParams(
            dimension_semantics=("arbitrary", "arbitrary"), vmem_limit_bytes=VMEM_LIMIT),
        name="gla",
    )(q, p, p, p, glr, wup, b_alpha, gnw)


def _merge_kernel(og_ref, gu_ref, gv_ref, gate0_ref, gate1_ref, x_ref, ws_ref, bs_ref,
                  wb0_ref, wb1_ref, wo_ref, h_ref, gm_ref):
    pos_t = lax.broadcasted_iota(jnp.int32, (GMLP_BLOCK, GMLP_BLOCK), 0) // CHUNK
    pos_s = lax.broadcasted_iota(jnp.int32, (GMLP_BLOCK, GMLP_BLOCK), 1) // CHUNK
    causal = pos_t >= pos_s
    for g in range(GMLP_GROUPS):
        w = jnp.where(causal, ws_ref[g], 0.0).astype(BF16)
        cols = slice(g * GMLP_DG, (g + 1) * GMLP_DG)
        for blk in range(MERGE_TM // GMLP_BLOCK):
            rows = pl.ds(blk * GMLP_BLOCK, GMLP_BLOCK)
            mixed = jnp.dot(w, gv_ref[rows, cols], preferred_element_type=F32) + bs_ref[g]
            gm_ref[rows, cols] = (gu_ref[rows, cols].astype(F32) * mixed).astype(BF16)

    d0 = jnp.dot(og_ref[...], wb0_ref[...], preferred_element_type=F32)
    d1 = jnp.dot(gm_ref[...], wb1_ref[...], preferred_element_type=F32)
    mixed = (gate0_ref[...].astype(F32) * d0 + gate1_ref[...].astype(F32) * d1).astype(BF16)
    h_ref[...] = x_ref[...] + jnp.dot(mixed, wo_ref[...], preferred_element_type=F32)


def _merge(o_gla, p, x2, w_spatial, b_sp, wb0, wb1, wo):
    m = x2.shape[0]
    const2 = lambda i: (0, 0)
    const3 = lambda i: (0, 0, 0)
    single = pl.Buffered(1)
    return pl.pallas_call(
        _merge_kernel,
        grid=(m // MERGE_TM,),
        in_specs=[
            pl.BlockSpec((MERGE_TM, V_W), lambda i: (i, 0)),
            pl.BlockSpec((MERGE_TM, D_MODEL), lambda i: (i, COL_GU // D_MODEL)),
            pl.BlockSpec((MERGE_TM, D_MODEL), lambda i: (i, COL_GV // D_MODEL)),
            pl.BlockSpec((MERGE_TM, D_MODEL), lambda i: (i, COL_GATE // D_MODEL)),
            pl.BlockSpec((MERGE_TM, D_MODEL), lambda i: (i, COL_GATE // D_MODEL + 1)),
            pl.BlockSpec((MERGE_TM, D_MODEL), lambda i: (i, 0)),
            pl.BlockSpec((GMLP_GROUPS, GMLP_BLOCK, GMLP_BLOCK), const3, pipeline_mode=single),
            pl.BlockSpec((GMLP_GROUPS, GMLP_BLOCK, GMLP_DG), const3, pipeline_mode=single),
            pl.BlockSpec((V_W, D_MODEL), const2, pipeline_mode=single),
            pl.BlockSpec((V_W, D_MODEL), const2, pipeline_mode=single),
            pl.BlockSpec((D_MODEL, D_MODEL), const2, pipeline_mode=single),
        ],
        out_specs=pl.BlockSpec((MERGE_TM, D_MODEL), lambda i: (i, 0)),
        out_shape=jax.ShapeDtypeStruct((m, D_MODEL), F32),
        scratch_shapes=[pltpu.VMEM((MERGE_TM, D_MODEL), BF16)],
        compiler_params=pltpu.CompilerParams(
            dimension_semantics=("arbitrary",), vmem_limit_bytes=VMEM_LIMIT),
        name="merge",
    )(o_gla, p, p, p, p, x2, w_spatial, b_sp, wb0, wb1, wo)


def _ffn_kernel(h_ref, nw_ref, wu_ref, wd_ref, fw_ref, y_ref, hn_ref, acc_ref):
    f = pl.program_id(1)

    @pl.when(f == 0)
    def _():
        h = h_ref[...]
        ms = jnp.mean(h * h, axis=-1, keepdims=True)
        hn_ref[...] = (h * lax.rsqrt(ms + EPS) * nw_ref[...]).astype(BF16)
        acc_ref[...] = h

    up = jnp.dot(hn_ref[...], wu_ref[...], preferred_element_type=F32)
    up = jnp.square(jnp.maximum(up, 0.0)).astype(BF16)
    acc_ref[...] += jnp.dot(up, wd_ref[...], preferred_element_type=F32)

    @pl.when(f == pl.num_programs(1) - 1)
    def _():
        h2 = acc_ref[...]
        ms = jnp.mean(h2 * h2, axis=-1, keepdims=True)
        y_ref[...] = h2 * lax.rsqrt(ms + EPS) * fw_ref[...]


def _ffn(h, norm_w, wu, wd, final_w):
    m = h.shape[0]
    return pl.pallas_call(
        _ffn_kernel,
        grid=(m // FFN_TM, D_FF // FFN_TF),
        in_specs=[
            pl.BlockSpec((FFN_TM, D_MODEL), lambda i, f: (i, 0)),
            pl.BlockSpec((1, D_MODEL), lambda i, f: (0, 0)),
            pl.BlockSpec((D_MODEL, FFN_TF), lambda i, f: (0, f)),
            pl.BlockSpec((FFN_TF, D_MODEL), lambda i, f: (f, 0)),
            pl.BlockSpec((1, D_MODEL), lambda i, f: (0, 0)),
        ],
        out_specs=pl.BlockSpec((FFN_TM, D_MODEL), lambda i, f: (i, 0)),
        out_shape=jax.ShapeDtypeStruct((m, D_MODEL), F32),
        scratch_shapes=[
            pltpu.VMEM((FFN_TM, D_MODEL), BF16),
            pltpu.VMEM((FFN_TM, D_MODEL), F32),
        ],
        compiler_params=pltpu.CompilerParams(
            dimension_semantics=("arbitrary", "arbitrary"), vmem_limit_bytes=VMEM_LIMIT),
        name="ffn",
    )(h, norm_w, wu, wd, final_w)


def _layer(h2, batch, seq, norm_mix_w, w_in, w_alpha_up, b_alpha, gla_norm_w, gmlp_ln_w,
           gmlp_ln_b, w_spatial, b_spatial, b_gate, w_branch, w_out, norm_mlp_w, w_ff_up,
           w_ff_down, final_w):
    w_in_t = w_in.T
    xn, q, glr = _qnorm(h2, norm_mix_w.reshape(1, D_MODEL), w_in_t)
    p = _in_proj(xn, w_in_t, b_gate.reshape(1, 2 * D_MODEL), gmlp_ln_w.reshape(1, GMLP_DG),
                 gmlp_ln_b.reshape(1, GMLP_DG))
    o_gla = _gla(q, p, glr, w_alpha_up.astype(BF16), b_alpha.reshape(1, QK_W),
                 gla_norm_w.reshape(1, GLA_DV), batch, seq)
    b_sp = jnp.broadcast_to(b_spatial[:, :, None], (GMLP_GROUPS, GMLP_BLOCK, GMLP_DG))
    h_mid = _merge(o_gla, p, h2, w_spatial, b_sp, w_branch[0].astype(BF16),
                   w_branch[1].astype(BF16), w_out.astype(BF16))
    return _ffn(h_mid, norm_mlp_w.reshape(1, D_MODEL), w_ff_up.astype(BF16),
                w_ff_down.astype(BF16), final_w)


def kernel(x, norm_mix_w, w_in, w_alpha_up, b_alpha, gla_norm_w, gmlp_ln_w, gmlp_ln_b,
           w_spatial, b_spatial, b_gate, w_branch, w_out, norm_mlp_w, w_ff_up, w_ff_down,
           norm_final_w):
    batch, seq, d = x.shape
    depth = w_in.shape[0]
    assert d == D_MODEL and depth == 1 and seq % GLA_TB == 0
    assert w_in.shape[2] == D_IN
    h2 = x.reshape(batch * seq, d)
    y = _layer(h2, batch, seq, norm_mix_w[0], w_in[0], w_alpha_up[0], b_alpha[0],
               gla_norm_w[0], gmlp_ln_w[0], gmlp_ln_b[0], w_spatial[0], b_spatial[0],
               b_gate[0], w_branch[0], w_out[0], norm_mlp_w[0], w_ff_up[0], w_ff_down[0],
               norm_final_w.reshape(1, D_MODEL))
    return y.reshape(batch, seq, d)
```

```python
import jax
import jax.numpy as jnp
from jax import lax
from jax.experimental import pallas as pl
from jax.experimental.pallas import tpu as pltpu

D_MODEL = 2048
CHUNK = 64
GLA_HEADS = 4
GLA_DK = D_MODEL // 8
GLA_DV = D_MODEL // 4
GLA_GATE_RANK = 16
GLA_TAU = 16.0
GMLP_BLOCK = 128
GMLP_GROUPS = 8
GMLP_DG = D_MODEL // GMLP_GROUPS
D_FF = 4 * D_MODEL
EPS = 1e-6
QK_W = GLA_HEADS * GLA_DK
V_W = GLA_HEADS * GLA_DV
D_IN = 2 * QK_W + 2 * V_W + GLA_GATE_RANK + 4 * D_MODEL

F32 = jnp.float32
BF16 = jnp.bfloat16

LANES = 128

WCOL_Q = 0
WCOL_K = QK_W
WCOL_V = 2 * QK_W
WCOL_R = WCOL_V + V_W
WCOL_GLR = WCOL_R + V_W
WCOL_GU = WCOL_GLR + GLA_GATE_RANK

COL_V = 0
COL_R = COL_V + V_W
COL_GU = COL_R + V_W
COL_GV = COL_GU + D_MODEL
COL_GATE = COL_GV + D_MODEL
COL_K = COL_GATE + 2 * D_MODEL
P_WIDTH = COL_K + QK_W

V7X_VMEM_BYTES = 64 * 1024 * 1024
VMEM_LIMIT = V7X_VMEM_BYTES - 8 * 1024 * 1024

IN_TM = 1024
IN_TN = 1024
IN_RC = 256
CAST_RC = 256
GLA_TB = 512
MERGE_TM = 256
FFN_TM = 512
FFN_TF = 1024


def _sigmoid(x):
    return 0.5 * (1.0 + jnp.tanh(0.5 * x))


def _gelu(x):
    return jax.nn.gelu(x, approximate=True)


def _cast_transposed(wt_ref, wb_ref, n_rows):
    for rc in range(n_rows // CAST_RC):
        rows = pl.ds(rc * CAST_RC, CAST_RC)
        wb_ref[:, rc * CAST_RC:(rc + 1) * CAST_RC] = wt_ref[rows, :].T.astype(BF16)


def _qnorm_kernel(x_ref, nw_ref, wq_ref, wg_ref, xn_ref, q_ref, glr_ref, wqb_ref, wgb_ref):
    @pl.when(pl.program_id(0) == 0)
    def _():
        _cast_transposed(wq_ref, wqb_ref, QK_W)
        wgb_ref[...] = wg_ref[...].T.astype(BF16)

    for c in range(IN_TM // IN_RC):
        rows = pl.ds(c * IN_RC, IN_RC)
        x = x_ref[rows, :]
        ms = jnp.mean(x * x, axis=-1, keepdims=True)
        xn = (x * lax.rsqrt(ms + EPS) * nw_ref[...]).astype(BF16)
        xn_ref[rows, :] = xn
        q = jnp.dot(xn, wqb_ref[...], preferred_element_type=F32)
        q_ref[rows, :] = (q * (GLA_DK ** -0.5)).astype(BF16)
        glr_ref[rows, :] = jnp.dot(xn, wgb_ref[...], preferred_element_type=F32)


def _qnorm(x2, norm_w, w_in_t):
    m = x2.shape[0]
    single = pl.Buffered(1)
    return pl.pallas_call(
        _qnorm_kernel,
        grid=(m // IN_TM,),
        in_specs=[
            pl.BlockSpec((IN_TM, D_MODEL), lambda i: (i, 0)),
            pl.BlockSpec((1, D_MODEL), lambda i: (0, 0)),
            pl.BlockSpec((QK_W, D_MODEL), lambda i: (WCOL_Q // QK_W, 0), pipeline_mode=single),
            pl.BlockSpec((LANES, D_MODEL), lambda i: (WCOL_GLR // LANES, 0), pipeline_mode=single),
        ],
        out_specs=[
            pl.BlockSpec((IN_TM, D_MODEL), lambda i: (i, 0)),
            pl.BlockSpec((IN_TM, QK_W), lambda i: (i, 0)),
            pl.BlockSpec((IN_TM, LANES), lambda i: (i, 0)),
        ],
        out_shape=[
            jax.ShapeDtypeStruct((m, D_MODEL), BF16),
            jax.ShapeDtypeStruct((m, QK_W), BF16),
            jax.ShapeDtypeStruct((m, LANES), F32),
        ],
        scratch_shapes=[pltpu.VMEM((D_MODEL, QK_W), BF16), pltpu.VMEM((D_MODEL, LANES), BF16)],
        compiler_params=pltpu.CompilerParams(
            dimension_semantics=("arbitrary",), vmem_limit_bytes=VMEM_LIMIT),
        name="qnorm",
    )(x2, norm_w, w_in_t, w_in_t)


_T_R, _T_GU, _T_GV, _T_GATE, _T_K = (c // IN_TN for c in (COL_R, COL_GU, COL_GV, COL_GATE, COL_K))
_N_T = P_WIDTH // IN_TN
_N_SIDE = 32
_WU_ROWS = D_MODEL // _N_SIDE
_WD_ROWS = D_FF // _N_SIDE


def _in_proj_kernel(xn_ref, w_ref, bg_ref, lnw_ref, lnb_ref, wu_ref, wd_ref,
                    p_ref, wub_ref, wdb_ref, wb_ref):
    t = pl.program_id(0)
    i = pl.program_id(1)

    @pl.when(i == 0)
    def _():
        _cast_transposed(w_ref, wb_ref, IN_TN)

    def run(epilogue):
        wub_ref[...] = wu_ref[...].astype(BF16)
        wdb_ref[...] = wd_ref[...].astype(BF16)
        for c in range(IN_TM // IN_RC):
            rows = pl.ds(c * IN_RC, IN_RC)
            acc = jnp.dot(xn_ref[rows, :], wb_ref[...], preferred_element_type=F32)
            p_ref[rows, :] = epilogue(acc).astype(BF16)

    def epi_ln(acc):
        outs = []
        for g in range(IN_TN // GMLP_DG):
            cols = slice(g * GMLP_DG, (g + 1) * GMLP_DG)
            v = _gelu(acc[:, cols])
            mu = jnp.mean(v, axis=-1, keepdims=True)
            vc = v - mu
            var = jnp.mean(vc * vc, axis=-1, keepdims=True)
            outs.append(vc * lax.rsqrt(var + EPS) * lnw_ref[...] + lnb_ref[...])
        return jnp.concatenate(outs, axis=-1)

    @pl.when(jnp.logical_or(t < _T_R, t >= _T_K))
    def _():
        run(lambda a: a)

    @pl.when(jnp.logical_and(t >= _T_R, t < _T_GU))
    def _():
        run(lambda a: a * _sigmoid(a))

    @pl.when(jnp.logical_and(t >= _T_GU, t < _T_GV))
    def _():
        run(_gelu)

    @pl.when(jnp.logical_and(t >= _T_GV, t < _T_GATE))
    def _():
        run(epi_ln)

    @pl.when(jnp.logical_and(t >= _T_GATE, t < _T_K))
    def _():
        run(lambda a: _sigmoid(a + bg_ref[...]))


def _in_proj(xn, w_in_t, b_gate, ln_w, ln_b, w_ff_up, w_ff_down):
    m = xn.shape[0]
    n_i = m // IN_TM
    n_gate_blocks = (2 * D_MODEL) // IN_TN
    assert _N_T * n_i >= 2 * _N_SIDE

    def w_rows(t, i):
        col = jnp.where(t < _T_GU, WCOL_V + t * IN_TN,
                        jnp.where(t < _T_K, WCOL_GU + (t - _T_GU) * IN_TN, WCOL_K))
        return (pl.multiple_of(col, GLA_GATE_RANK), 0)

    step = lambda t, i: t * n_i + i
    wu_blk = lambda t, i: (jnp.minimum(step(t, i), _N_SIDE - 1), 0)
    wd_blk = lambda t, i: (jnp.clip(step(t, i) - _N_SIDE, 0, _N_SIDE - 1), 0)
    return pl.pallas_call(
        _in_proj_kernel,
        grid=(_N_T, n_i),
        in_specs=[
            pl.BlockSpec((IN_TM, D_MODEL), lambda t, i: (i, 0)),
            pl.BlockSpec((pl.Element(IN_TN), pl.Element(D_MODEL)), w_rows),
            pl.BlockSpec((1, IN_TN),
                         lambda t, i: (0, jnp.clip(t - _T_GATE, 0, n_gate_blocks - 1))),
            pl.BlockSpec((1, GMLP_DG), lambda t, i: (0, 0)),
            pl.BlockSpec((1, GMLP_DG), lambda t, i: (0, 0)),
            pl.BlockSpec((_WU_ROWS, D_FF), wu_blk),
            pl.BlockSpec((_WD_ROWS, D_MODEL), wd_blk),
        ],
        out_specs=[
            pl.BlockSpec((IN_TM, IN_TN), lambda t, i: (i, t)),
            pl.BlockSpec((_WU_ROWS, D_FF), wu_blk),
            pl.BlockSpec((_WD_ROWS, D_MODEL), wd_blk),
        ],
        out_shape=[
            jax.ShapeDtypeStruct((m, P_WIDTH), BF16),
            jax.ShapeDtypeStruct((D_MODEL, D_FF), BF16),
            jax.ShapeDtypeStruct((D_FF, D_MODEL), BF16),
        ],
        scratch_shapes=[pltpu.VMEM((D_MODEL, IN_TN), BF16)],
        compiler_params=pltpu.CompilerParams(
            dimension_semantics=("arbitrary", "arbitrary"), vmem_limit_bytes=VMEM_LIMIT),
        name="in_proj",
    )(xn, w_in_t, b_gate, ln_w, ln_b, w_ff_up, w_ff_down)


def _gla_kernel(q_ref, k_ref, v_ref, rs_ref, glr_ref, wup_ref, ba_ref, gnw_ref,
                wb0_ref, wb1_ref, wo_ref,
                o_ref, wb0b_ref, wb1b_ref, wob_ref, state_ref, la_ref):
    t = pl.program_id(1)

    @pl.when(t == 0)
    def _():
        state_ref[...] = jnp.zeros_like(state_ref)

    wb0b_ref[...] = wb0_ref[...].astype(BF16)
    wb1b_ref[...] = wb1_ref[...].astype(BF16)
    wob_ref[...] = wo_ref[...].astype(BF16)

    glr = glr_ref[:, :GLA_GATE_RANK].astype(BF16)
    z = jnp.dot(glr, wup_ref[...], preferred_element_type=F32) + ba_ref[...]
    la_ref[...] = (jnp.minimum(z, 0.0) - jnp.log1p(jnp.exp(-jnp.abs(z)))) * (1.0 / GLA_TAU)

    row = lax.broadcasted_iota(jnp.int32, (CHUNK, CHUNK), 0)
    col = lax.broadcasted_iota(jnp.int32, (CHUNK, CHUNK), 1)
    tri = (row >= col).astype(BF16)

    def chunk_body(c, carry):
        rows = pl.ds(pl.multiple_of(c * CHUNK, CHUNK), CHUNK)
        la = la_ref[rows, :]
        la_hi = la.astype(BF16)
        la_lo = (la - la_hi.astype(F32)).astype(BF16)
        lcum = (jnp.dot(tri, la_hi, preferred_element_type=F32)
                + jnp.dot(tri, la_lo, preferred_element_type=F32))
        l_end = lcum[CHUNK - 1:CHUNK, :]
        k_dec = (k_ref[rows, :].astype(F32) * jnp.exp(l_end - lcum)).astype(BF16)
        decay = jnp.exp(l_end)
        q = q_ref[rows, :]
        v = v_ref[rows, :]
        rs = rs_ref[rows, :]
        for h in range(GLA_HEADS):
            kc = slice(h * GLA_DK, (h + 1) * GLA_DK)
            vc = slice(h * GLA_DV, (h + 1) * GLA_DV)
            kv_t = lax.dot_general(v[:, vc], k_dec[:, kc], (((0,), (0,)), ((), ())),
                                   preferred_element_type=F32)
            s = state_ref[h] * decay[:, kc] + kv_t
            state_ref[h] = s
            o = lax.dot_general(q[:, kc], s.astype(BF16), (((1,), (1,)), ((), ())),
                                preferred_element_type=F32)
            ms = jnp.mean(o * o, axis=-1, keepdims=True)
            o = o * lax.rsqrt(ms + EPS) * gnw_ref[...] * rs[:, vc].astype(F32)
            o_ref[rows, vc] = o.astype(BF16)
        return carry

    lax.fori_loop(0, GLA_TB // CHUNK, chunk_body, 0)


def _gla(q, p, glr, wup, b_alpha, gnw, w_branch, w_out, batch, seq):
    m = p.shape[0]
    nt = seq // GLA_TB
    rowblk = lambda b, t: b * nt + t
    w_rows = D_MODEL // (batch * nt)
    w_spec = pl.BlockSpec((w_rows, D_MODEL), lambda b, t: (rowblk(b, t), 0))
    wbr_spec = lambda n: pl.BlockSpec((None, w_rows, D_MODEL), lambda b, t: (n, rowblk(b, t), 0))
    w_shape = jax.ShapeDtypeStruct((D_MODEL, D_MODEL), BF16)
    return pl.pallas_call(
        _gla_kernel,
        grid=(batch, nt),
        in_specs=[
            pl.BlockSpec((GLA_TB, QK_W), lambda b, t: (rowblk(b, t), 0)),
            pl.BlockSpec((GLA_TB, QK_W), lambda b, t: (rowblk(b, t), COL_K // QK_W)),
            pl.BlockSpec((GLA_TB, V_W), lambda b, t: (rowblk(b, t), COL_V // V_W)),
            pl.BlockSpec((GLA_TB, V_W), lambda b, t: (rowblk(b, t), COL_R // V_W)),
            pl.BlockSpec((GLA_TB, LANES), lambda b, t: (rowblk(b, t), 0)),
            pl.BlockSpec((GLA_GATE_RANK, QK_W), lambda b, t: (0, 0)),
            pl.BlockSpec((1, QK_W), lambda b, t: (0, 0)),
            pl.BlockSpec((1, GLA_DV), lambda b, t: (0, 0)),
            wbr_spec(0), wbr_spec(1), w_spec,
        ],
        out_specs=[pl.BlockSpec((GLA_TB, V_W), lambda b, t: (rowblk(b, t), 0)),
                   w_spec, w_spec, w_spec],
        out_shape=[jax.ShapeDtypeStruct((m, V_W), BF16), w_shape, w_shape, w_shape],
        scratch_shapes=[
            pltpu.VMEM((GLA_HEADS, GLA_DV, GLA_DK), F32),
            pltpu.VMEM((GLA_TB, QK_W), F32),
        ],
        compiler_params=pltpu.CompilerParams(
            dimension_semantics=("arbitrary", "arbitrary"), vmem_limit_bytes=VMEM_LIMIT),
        name="gla",
    )(q, p, p, p, glr, wup, b_alpha, gnw, w_branch, w_branch, w_out)


def _merge_kernel(og_ref, gu_ref, gv_ref, gate0_ref, gate1_ref, x_ref, ws_ref, bs_ref,
                  wb0_ref, wb1_ref, wo_ref, h_ref, gm_ref):
    pos_t = lax.broadcasted_iota(jnp.int32, (GMLP_BLOCK, GMLP_BLOCK), 0) // CHUNK
    pos_s = lax.broadcasted_iota(jnp.int32, (GMLP_BLOCK, GMLP_BLOCK), 1) // CHUNK
    causal = pos_t >= pos_s
    for g in range(GMLP_GROUPS):
        w = jnp.where(causal, ws_ref[g], 0.0).astype(BF16)
        cols = slice(g * GMLP_DG, (g + 1) * GMLP_DG)
        for blk in range(MERGE_TM // GMLP_BLOCK):
            rows = pl.ds(blk * GMLP_BLOCK, GMLP_BLOCK)
            mixed = jnp.dot(w, gv_ref[rows, cols], preferred_element_type=F32) + bs_ref[g]
            gm_ref[rows, cols] = (gu_ref[rows, cols].astype(F32) * mixed).astype(BF16)

    d0 = jnp.dot(og_ref[...], wb0_ref[...], preferred_element_type=F32)
    d1 = jnp.dot(gm_ref[...], wb1_ref[...], preferred_element_type=F32)
    mixed = (gate0_ref[...].astype(F32) * d0 + gate1_ref[...].astype(F32) * d1).astype(BF16)
    h_ref[...] = x_ref[...] + jnp.dot(mixed, wo_ref[...], preferred_element_type=F32)


def _merge(o_gla, p, x2, w_spatial, b_sp, wb0, wb1, wo):
    m = x2.shape[0]
    const2 = lambda i: (0, 0)
    const3 = lambda i: (0, 0, 0)
    single = pl.Buffered(1)
    return pl.pallas_call(
        _merge_kernel,
        grid=(m // MERGE_TM,),
        in_specs=[
            pl.BlockSpec((MERGE_TM, V_W), lambda i: (i, 0)),
            pl.BlockSpec((MERGE_TM, D_MODEL), lambda i: (i, COL_GU // D_MODEL)),
            pl.BlockSpec((MERGE_TM, D_MODEL), lambda i: (i, COL_GV // D_MODEL)),
            pl.BlockSpec((MERGE_TM, D_MODEL), lambda i: (i, COL_GATE // D_MODEL)),
            pl.BlockSpec((MERGE_TM, D_MODEL), lambda i: (i, COL_GATE // D_MODEL + 1)),
            pl.BlockSpec((MERGE_TM, D_MODEL), lambda i: (i, 0)),
            pl.BlockSpec((GMLP_GROUPS, GMLP_BLOCK, GMLP_BLOCK), const3, pipeline_mode=single),
            pl.BlockSpec((GMLP_GROUPS, GMLP_BLOCK, GMLP_DG), const3, pipeline_mode=single),
            pl.BlockSpec((V_W, D_MODEL), const2, pipeline_mode=single),
            pl.BlockSpec((V_W, D_MODEL), const2, pipeline_mode=single),
            pl.BlockSpec((D_MODEL, D_MODEL), const2, pipeline_mode=single),
        ],
        out_specs=pl.BlockSpec((MERGE_TM, D_MODEL), lambda i: (i, 0)),
        out_shape=jax.ShapeDtypeStruct((m, D_MODEL), F32),
        scratch_shapes=[pltpu.VMEM((MERGE_TM, D_MODEL), BF16)],
        compiler_params=pltpu.CompilerParams(
            dimension_semantics=("arbitrary",), vmem_limit_bytes=VMEM_LIMIT),
        name="merge",
    )(o_gla, p, p, p, p, x2, w_spatial, b_sp, wb0, wb1, wo)


def _ffn_kernel(h_ref, nw_ref, wu_ref, wd_ref, fw_ref, y_ref, hn_ref, acc_ref):
    f = pl.program_id(1)

    @pl.when(f == 0)
    def _():
        h = h_ref[...]
        ms = jnp.mean(h * h, axis=-1, keepdims=True)
        hn_ref[...] = (h * lax.rsqrt(ms + EPS) * nw_ref[...]).astype(BF16)
        acc_ref[...] = h

    up = jnp.dot(hn_ref[...], wu_ref[...], preferred_element_type=F32)
    up = jnp.square(jnp.maximum(up, 0.0)).astype(BF16)
    acc_ref[...] += jnp.dot(up, wd_ref[...], preferred_element_type=F32)

    @pl.when(f == pl.num_programs(1) - 1)
    def _():
        h2 = acc_ref[...]
        ms = jnp.mean(h2 * h2, axis=-1, keepdims=True)
        y_ref[...] = h2 * lax.rsqrt(ms + EPS) * fw_ref[...]


def _ffn(h, norm_w, wu, wd, final_w):
    m = h.shape[0]
    return pl.pallas_call(
        _ffn_kernel,
        grid=(m // FFN_TM, D_FF // FFN_TF),
        in_specs=[
            pl.BlockSpec((FFN_TM, D_MODEL), lambda i, f: (i, 0)),
            pl.BlockSpec((1, D_MODEL), lambda i, f: (0, 0)),
            pl.BlockSpec((D_MODEL, FFN_TF), lambda i, f: (0, f)),
            pl.BlockSpec((FFN_TF, D_MODEL), lambda i, f: (f, 0)),
            pl.BlockSpec((1, D_MODEL), lambda i, f: (0, 0)),
        ],
        out_specs=pl.BlockSpec((FFN_TM, D_MODEL), lambda i, f: (i, 0)),
        out_shape=jax.ShapeDtypeStruct((m, D_MODEL), F32),
        scratch_shapes=[
            pltpu.VMEM((FFN_TM, D_MODEL), BF16),
            pltpu.VMEM((FFN_TM, D_MODEL), F32),
        ],
        compiler_params=pltpu.CompilerParams(
            dimension_semantics=("arbitrary", "arbitrary"), vmem_limit_bytes=VMEM_LIMIT),
        name="ffn",
    )(h, norm_w, wu, wd, final_w)


def _layer(h2, batch, seq, norm_mix_w, w_in, w_alpha_up, b_alpha, gla_norm_w, gmlp_ln_w,
           gmlp_ln_b, w_spatial, b_spatial, b_gate, w_branch, w_out, norm_mlp_w, w_ff_up,
           w_ff_down, final_w):
    w_in_t = w_in.T
    xn, q, glr = _qnorm(h2, norm_mix_w.reshape(1, D_MODEL), w_in_t)
    p, wu_b, wd_b = _in_proj(xn, w_in_t, b_gate.reshape(1, 2 * D_MODEL),
                             gmlp_ln_w.reshape(1, GMLP_DG), gmlp_ln_b.reshape(1, GMLP_DG),
                             w_ff_up, w_ff_down)
    o_gla, wb0_b, wb1_b, wo_b = _gla(q, p, glr, w_alpha_up.astype(BF16),
                                     b_alpha.reshape(1, QK_W), gla_norm_w.reshape(1, GLA_DV),
                                     w_branch, w_out, batch, seq)
    b_sp = jnp.broadcast_to(b_spatial[:, :, None], (GMLP_GROUPS, GMLP_BLOCK, GMLP_DG))
    h_mid = _merge(o_gla, p, h2, w_spatial, b_sp, wb0_b, wb1_b, wo_b)
    return _ffn(h_mid, norm_mlp_w.reshape(1, D_MODEL), wu_b, wd_b, final_w)


def kernel(x, norm_mix_w, w_in, w_alpha_up, b_alpha, gla_norm_w, gmlp_ln_w, gmlp_ln_b,
           w_spatial, b_spatial, b_gate, w_branch, w_out, norm_mlp_w, w_ff_up, w_ff_down,
           norm_final_w):
    batch, seq, d = x.shape
    depth = w_in.shape[0]
    assert d == D_MODEL and depth == 1 and seq % GLA_TB == 0
    assert w_in.shape[2] == D_IN
    h2 = x.reshape(batch * seq, d)
    y = _layer(h2, batch, seq, norm_mix_w[0], w_in[0], w_alpha_up[0], b_alpha[0],
               gla_norm_w[0], gmlp_ln_w[0], gmlp_ln_b[0], w_spatial[0], b_spatial[0],
               b_gate[0], w_branch[0], w_out[0], norm_mlp_w[0], w_ff_up[0], w_ff_down[0],
               norm_final_w.reshape(1, D_MODEL))
    return y.reshape(batch, seq, d)
```

```python
import jax
import jax.numpy as jnp
from jax import lax
from jax.experimental import pallas as pl
from jax.experimental.pallas import tpu as pltpu

D_MODEL = 2048
CHUNK = 64
GLA_HEADS = 4
GLA_DK = D_MODEL // 8
GLA_DV = D_MODEL // 4
GLA_GATE_RANK = 16
GLA_TAU = 16.0
GMLP_BLOCK = 128
GMLP_GROUPS = 8
GMLP_DG = D_MODEL // GMLP_GROUPS
D_FF = 4 * D_MODEL
EPS = 1e-6
QK_W = GLA_HEADS * GLA_DK
V_W = GLA_HEADS * GLA_DV
D_IN = 2 * QK_W + 2 * V_W + GLA_GATE_RANK + 4 * D_MODEL

F32 = jnp.float32
BF16 = jnp.bfloat16

LANES = 128

WCOL_Q = 0
WCOL_K = QK_W
WCOL_V = 2 * QK_W
WCOL_R = WCOL_V + V_W
WCOL_GLR = WCOL_R + V_W
WCOL_GU = WCOL_GLR + GLA_GATE_RANK

COL_V = 0
COL_R = COL_V + V_W
COL_GU = COL_R + V_W
COL_GV = COL_GU + D_MODEL
COL_GATE = COL_GV + D_MODEL
COL_K = COL_GATE + 2 * D_MODEL
P_WIDTH = COL_K + QK_W

V7X_VMEM_BYTES = 64 * 1024 * 1024
VMEM_LIMIT = V7X_VMEM_BYTES - 8 * 1024 * 1024

IN_TM = 1024
IN_TN = 1024
IN_RC = 256
CAST_RC = 256
GLA_TB = 512
GLA_UNROLL = 4
MERGE_TM = 256
FFN_TM = 512
FFN_TF = 1024


def _sigmoid(x):
    return 0.5 * (1.0 + jnp.tanh(0.5 * x))


def _gelu(x):
    return jax.nn.gelu(x, approximate=True)


def _cast_transposed(wt_ref, wb_ref, n_rows):
    for rc in range(n_rows // CAST_RC):
        rows = pl.ds(rc * CAST_RC, CAST_RC)
        wb_ref[:, rc * CAST_RC:(rc + 1) * CAST_RC] = wt_ref[rows, :].T.astype(BF16)


def _qnorm_kernel(x_ref, nw_ref, wq_ref, wg_ref, xn_ref, q_ref, glr_ref, wqb_ref, wgb_ref):
    @pl.when(pl.program_id(0) == 0)
    def _():
        _cast_transposed(wq_ref, wqb_ref, QK_W)
        wgb_ref[...] = wg_ref[...].T.astype(BF16)

    for c in range(IN_TM // IN_RC):
        rows = pl.ds(c * IN_RC, IN_RC)
        x = x_ref[rows, :]
        ms = jnp.mean(x * x, axis=-1, keepdims=True)
        xn = (x * lax.rsqrt(ms + EPS) * nw_ref[...]).astype(BF16)
        xn_ref[rows, :] = xn
        q = jnp.dot(xn, wqb_ref[...], preferred_element_type=F32)
        q_ref[rows, :] = (q * (GLA_DK ** -0.5)).astype(BF16)
        glr_ref[rows, :] = jnp.dot(xn, wgb_ref[...], preferred_element_type=F32)


def _qnorm(x2, norm_w, w_in_t):
    m = x2.shape[0]
    single = pl.Buffered(1)
    return pl.pallas_call(
        _qnorm_kernel,
        grid=(m // IN_TM,),
        in_specs=[
            pl.BlockSpec((IN_TM, D_MODEL), lambda i: (i, 0)),
            pl.BlockSpec((1, D_MODEL), lambda i: (0, 0)),
            pl.BlockSpec((QK_W, D_MODEL), lambda i: (WCOL_Q // QK_W, 0), pipeline_mode=single),
            pl.BlockSpec((LANES, D_MODEL), lambda i: (WCOL_GLR // LANES, 0), pipeline_mode=single),
        ],
        out_specs=[
            pl.BlockSpec((IN_TM, D_MODEL), lambda i: (i, 0)),
            pl.BlockSpec((IN_TM, QK_W), lambda i: (i, 0)),
            pl.BlockSpec((IN_TM, LANES), lambda i: (i, 0)),
        ],
        out_shape=[
            jax.ShapeDtypeStruct((m, D_MODEL), BF16),
            jax.ShapeDtypeStruct((m, QK_W), BF16),
            jax.ShapeDtypeStruct((m, LANES), F32),
        ],
        scratch_shapes=[pltpu.VMEM((D_MODEL, QK_W), BF16), pltpu.VMEM((D_MODEL, LANES), BF16)],
        compiler_params=pltpu.CompilerParams(
            dimension_semantics=("arbitrary",), vmem_limit_bytes=VMEM_LIMIT),
        name="qnorm",
    )(x2, norm_w, w_in_t, w_in_t)


_T_R, _T_GU, _T_GV, _T_GATE, _T_K = (c // IN_TN for c in (COL_R, COL_GU, COL_GV, COL_GATE, COL_K))
_N_T = P_WIDTH // IN_TN
_N_SIDE = 32
_WU_ROWS = D_MODEL // _N_SIDE
_WD_ROWS = D_FF // _N_SIDE


def _in_proj_kernel(xn_ref, w_ref, bg_ref, lnw_ref, lnb_ref, wu_ref, wd_ref,
                    p_ref, wub_ref, wdb_ref, wb_ref):
    t = pl.program_id(0)
    i = pl.program_id(1)

    @pl.when(i == 0)
    def _():
        _cast_transposed(w_ref, wb_ref, IN_TN)

    def run(epilogue):
        for c in range(IN_TM // IN_RC):
            rows = pl.ds(c * IN_RC, IN_RC)
            acc = jnp.dot(xn_ref[rows, :], wb_ref[...], preferred_element_type=F32)
            p_ref[rows, :] = epilogue(acc).astype(BF16)
        wub_ref[...] = wu_ref[...].astype(BF16)
        wdb_ref[...] = wd_ref[...].astype(BF16)

    def epi_ln(acc):
        outs = []
        for g in range(IN_TN // GMLP_DG):
            cols = slice(g * GMLP_DG, (g + 1) * GMLP_DG)
            v = _gelu(acc[:, cols])
            mu = jnp.mean(v, axis=-1, keepdims=True)
            vc = v - mu
            var = jnp.mean(vc * vc, axis=-1, keepdims=True)
            outs.append(vc * lax.rsqrt(var + EPS) * lnw_ref[...] + lnb_ref[...])
        return jnp.concatenate(outs, axis=-1)

    @pl.when(jnp.logical_or(t < _T_R, t >= _T_K))
    def _():
        run(lambda a: a)

    @pl.when(jnp.logical_and(t >= _T_R, t < _T_GU))
    def _():
        run(lambda a: a * _sigmoid(a))

    @pl.when(jnp.logical_and(t >= _T_GU, t < _T_GV))
    def _():
        run(_gelu)

    @pl.when(jnp.logical_and(t >= _T_GV, t < _T_GATE))
    def _():
        run(epi_ln)

    @pl.when(jnp.logical_and(t >= _T_GATE, t < _T_K))
    def _():
        run(lambda a: _sigmoid(a + bg_ref[...]))


def _in_proj(xn, w_in_t, b_gate, ln_w, ln_b, w_ff_up, w_ff_down):
    m = xn.shape[0]
    n_i = m // IN_TM
    n_gate_blocks = (2 * D_MODEL) // IN_TN
    assert _N_T * n_i >= 2 * _N_SIDE

    def w_rows(t, i):
        col = jnp.where(t < _T_GU, WCOL_V + t * IN_TN,
                        jnp.where(t < _T_K, WCOL_GU + (t - _T_GU) * IN_TN, WCOL_K))
        return (pl.multiple_of(col, GLA_GATE_RANK), 0)

    step = lambda t, i: t * n_i + i
    wu_blk = lambda t, i: (jnp.minimum(step(t, i), _N_SIDE - 1), 0)
    wd_blk = lambda t, i: (jnp.clip(step(t, i) - _N_SIDE, 0, _N_SIDE - 1), 0)
    return pl.pallas_call(
        _in_proj_kernel,
        grid=(_N_T, n_i),
        in_specs=[
            pl.BlockSpec((IN_TM, D_MODEL), lambda t, i: (i, 0)),
            pl.BlockSpec((pl.Element(IN_TN), pl.Element(D_MODEL)), w_rows),
            pl.BlockSpec((1, IN_TN),
                         lambda t, i: (0, jnp.clip(t - _T_GATE, 0, n_gate_blocks - 1))),
            pl.BlockSpec((1, GMLP_DG), lambda t, i: (0, 0)),
            pl.BlockSpec((1, GMLP_DG), lambda t, i: (0, 0)),
            pl.BlockSpec((_WU_ROWS, D_FF), wu_blk),
            pl.BlockSpec((_WD_ROWS, D_MODEL), wd_blk),
        ],
        out_specs=[
            pl.BlockSpec((IN_TM, IN_TN), lambda t, i: (i, t)),
            pl.BlockSpec((_WU_ROWS, D_FF), wu_blk),
            pl.BlockSpec((_WD_ROWS, D_MODEL), wd_blk),
        ],
        out_shape=[
            jax.ShapeDtypeStruct((m, P_WIDTH), BF16),
            jax.ShapeDtypeStruct((D_MODEL, D_FF), BF16),
            jax.ShapeDtypeStruct((D_FF, D_MODEL), BF16),
        ],
        scratch_shapes=[pltpu.VMEM((D_MODEL, IN_TN), BF16)],
        compiler_params=pltpu.CompilerParams(
            dimension_semantics=("arbitrary", "arbitrary"), vmem_limit_bytes=VMEM_LIMIT),
        name="in_proj",
    )(xn, w_in_t, b_gate, ln_w, ln_b, w_ff_up, w_ff_down)


def _gla_kernel(q_ref, k_ref, v_ref, rs_ref, glr_ref, wup_ref, ba_ref, gnw_ref,
                wb0_ref, wb1_ref, wo_ref,
                o_ref, wb0b_ref, wb1b_ref, wob_ref, state_ref, z_ref):
    t = pl.program_id(1)

    @pl.when(t == 0)
    def _():
        state_ref[...] = jnp.zeros_like(state_ref)

    glr = glr_ref[:, :GLA_GATE_RANK].astype(BF16)
    z_ref[...] = jnp.dot(glr, wup_ref[...], preferred_element_type=F32) + ba_ref[...]

    row = lax.broadcasted_iota(jnp.int32, (CHUNK, CHUNK), 0)
    col = lax.broadcasted_iota(jnp.int32, (CHUNK, CHUNK), 1)
    tri = (row >= col).astype(BF16)

    n_chunks = GLA_TB // CHUNK
    w_rows = wb0_ref.shape[0] // n_chunks

    def chunk_body(c, carry):
        wr = pl.ds(pl.multiple_of(c * w_rows, w_rows), w_rows)
        wb0b_ref[wr, :] = wb0_ref[wr, :].astype(BF16)
        wb1b_ref[wr, :] = wb1_ref[wr, :].astype(BF16)
        wob_ref[wr, :] = wo_ref[wr, :].astype(BF16)

        rows = pl.ds(pl.multiple_of(c * CHUNK, CHUNK), CHUNK)
        z = z_ref[rows, :]
        la = (jnp.minimum(z, 0.0) - jnp.log1p(jnp.exp(-jnp.abs(z)))) * (1.0 / GLA_TAU)
        la_hi = la.astype(BF16)
        la_lo = (la - la_hi.astype(F32)).astype(BF16)
        lcum = (jnp.dot(tri, la_hi, preferred_element_type=F32)
                + jnp.dot(tri, la_lo, preferred_element_type=F32))
        l_end = lcum[CHUNK - 1:CHUNK, :]
        k_dec = (k_ref[rows, :].astype(F32) * jnp.exp(l_end - lcum)).astype(BF16)
        decay = jnp.exp(l_end)
        q = q_ref[rows, :]
        v = v_ref[rows, :]
        rs = rs_ref[rows, :]
        for h in range(GLA_HEADS):
            kc = slice(h * GLA_DK, (h + 1) * GLA_DK)
            vc = slice(h * GLA_DV, (h + 1) * GLA_DV)
            kv_t = lax.dot_general(v[:, vc], k_dec[:, kc], (((0,), (0,)), ((), ())),
                                   preferred_element_type=F32)
            s = state_ref[h] * decay[:, kc] + kv_t
            state_ref[h] = s
            o = lax.dot_general(q[:, kc], s.astype(BF16), (((1,), (1,)), ((), ())),
                                preferred_element_type=F32)
            ms = jnp.mean(o * o, axis=-1, keepdims=True)
            o = o * lax.rsqrt(ms + EPS) * gnw_ref[...] * rs[:, vc].astype(F32)
            o_ref[rows, vc] = o.astype(BF16)
        return carry

    lax.fori_loop(0, n_chunks, chunk_body, 0, unroll=GLA_UNROLL)


def _gla(q, p, glr, wup, b_alpha, gnw, w_branch, w_out, batch, seq):
    m = p.shape[0]
    nt = seq // GLA_TB
    rowblk = lambda b, t: b * nt + t
    w_rows = D_MODEL // (batch * nt)
    w_spec = pl.BlockSpec((w_rows, D_MODEL), lambda b, t: (rowblk(b, t), 0))
    wbr_spec = lambda n: pl.BlockSpec((None, w_rows, D_MODEL), lambda b, t: (n, rowblk(b, t), 0))
    w_shape = jax.ShapeDtypeStruct((D_MODEL, D_MODEL), BF16)
    return pl.pallas_call(
        _gla_kernel,
        grid=(batch, nt),
        in_specs=[
            pl.BlockSpec((GLA_TB, QK_W), lambda b, t: (rowblk(b, t), 0)),
            pl.BlockSpec((GLA_TB, QK_W), lambda b, t: (rowblk(b, t), COL_K // QK_W)),
            pl.BlockSpec((GLA_TB, V_W), lambda b, t: (rowblk(b, t), COL_V // V_W)),
            pl.BlockSpec((GLA_TB, V_W), lambda b, t: (rowblk(b, t), COL_R // V_W)),
            pl.BlockSpec((GLA_TB, LANES), lambda b, t: (rowblk(b, t), 0)),
            pl.BlockSpec((GLA_GATE_RANK, QK_W), lambda b, t: (0, 0)),
            pl.BlockSpec((1, QK_W), lambda b, t: (0, 0)),
            pl.BlockSpec((1, GLA_DV), lambda b, t: (0, 0)),
            wbr_spec(0), wbr_spec(1), w_spec,
        ],
        out_specs=[pl.BlockSpec((GLA_TB, V_W), lambda b, t: (rowblk(b, t), 0)),
                   w_spec, w_spec, w_spec],
        out_shape=[jax.ShapeDtypeStruct((m, V_W), BF16), w_shape, w_shape, w_shape],
        scratch_shapes=[
            pltpu.VMEM((GLA_HEADS, GLA_DV, GLA_DK), F32),
            pltpu.VMEM((GLA_TB, QK_W), F32),
        ],
        compiler_params=pltpu.CompilerParams(
            dimension_semantics=("arbitrary", "arbitrary"), vmem_limit_bytes=VMEM_LIMIT),
        name="gla",
    )(q, p, p, p, glr, wup, b_alpha, gnw, w_branch, w_branch, w_out)


def _merge_kernel(og_ref, gu_ref, gv_ref, gate0_ref, gate1_ref, x_ref, ws_ref, bs_ref,
                  wb0_ref, wb1_ref, wo_ref, h_ref, gm_ref):
    pos_t = lax.broadcasted_iota(jnp.int32, (GMLP_BLOCK, GMLP_BLOCK), 0) // CHUNK
    pos_s = lax.broadcasted_iota(jnp.int32, (GMLP_BLOCK, GMLP_BLOCK), 1) // CHUNK
    causal = pos_t >= pos_s
    for g in range(GMLP_GROUPS):
        w = jnp.where(causal, ws_ref[g], 0.0).astype(BF16)
        cols = slice(g * GMLP_DG, (g + 1) * GMLP_DG)
        for blk in range(MERGE_TM // GMLP_BLOCK):
            rows = pl.ds(blk * GMLP_BLOCK, GMLP_BLOCK)
            mixed = jnp.dot(w, gv_ref[rows, cols], preferred_element_type=F32) + bs_ref[g]
            gm_ref[rows, cols] = (gu_ref[rows, cols].astype(F32) * mixed).astype(BF16)

    d0 = jnp.dot(og_ref[...], wb0_ref[...], preferred_element_type=F32)
    d1 = jnp.dot(gm_ref[...], wb1_ref[...], preferred_element_type=F32)
    mixed = (gate0_ref[...].astype(F32) * d0 + gate1_ref[...].astype(F32) * d1).astype(BF16)
    h_ref[...] = x_ref[...] + jnp.dot(mixed, wo_ref[...], preferred_element_type=F32)


def _merge(o_gla, p, x2, w_spatial, b_sp, wb0, wb1, wo):
    m = x2.shape[0]
    const2 = lambda i: (0, 0)
    const3 = lambda i: (0, 0, 0)
    single = pl.Buffered(1)
    return pl.pallas_call(
        _merge_kernel,
        grid=(m // MERGE_TM,),
        in_specs=[
            pl.BlockSpec((MERGE_TM, V_W), lambda i: (i, 0)),
            pl.BlockSpec((MERGE_TM, D_MODEL), lambda i: (i, COL_GU // D_MODEL)),
            pl.BlockSpec((MERGE_TM, D_MODEL), lambda i: (i, COL_GV // D_MODEL)),
            pl.BlockSpec((MERGE_TM, D_MODEL), lambda i: (i, COL_GATE // D_MODEL)),
            pl.BlockSpec((MERGE_TM, D_MODEL), lambda i: (i, COL_GATE // D_MODEL + 1)),
            pl.BlockSpec((MERGE_TM, D_MODEL), lambda i: (i, 0)),
            pl.BlockSpec((GMLP_GROUPS, GMLP_BLOCK, GMLP_BLOCK), const3, pipeline_mode=single),
            pl.BlockSpec((GMLP_GROUPS, GMLP_BLOCK, GMLP_DG), const3, pipeline_mode=single),
            pl.BlockSpec((V_W, D_MODEL), const2, pipeline_mode=single),
            pl.BlockSpec((V_W, D_MODEL), const2, pipeline_mode=single),
            pl.BlockSpec((D_MODEL, D_MODEL), const2, pipeline_mode=single),
        ],
        out_specs=pl.BlockSpec((MERGE_TM, D_MODEL), lambda i: (i, 0)),
        out_shape=jax.ShapeDtypeStruct((m, D_MODEL), F32),
        scratch_shapes=[pltpu.VMEM((MERGE_TM, D_MODEL), BF16)],
        compiler_params=pltpu.CompilerParams(
            dimension_semantics=("arbitrary",), vmem_limit_bytes=VMEM_LIMIT),
        name="merge",
    )(o_gla, p, p, p, p, x2, w_spatial, b_sp, wb0, wb1, wo)


def _ffn_kernel(h_ref, nw_ref, wu_ref, wd_ref, fw_ref, y_ref, hn_ref, acc_ref):
    f = pl.program_id(1)

    @pl.when(f == 0)
    def _():
        h = h_ref[...]
        ms = jnp.mean(h * h, axis=-1, keepdims=True)
        hn_ref[...] = (h * lax.rsqrt(ms + EPS) * nw_ref[...]).astype(BF16)
        acc_ref[...] = h

    up = jnp.dot(hn_ref[...], wu_ref[...], preferred_element_type=F32)
    up = jnp.square(jnp.maximum(up, 0.0)).astype(BF16)
    acc_ref[...] += jnp.dot(up, wd_ref[...], preferred_element_type=F32)

    @pl.when(f == pl.num_programs(1) - 1)
    def _():
        h2 = acc_ref[...]
        ms = jnp.mean(h2 * h2, axis=-1, keepdims=True)
        y_ref[...] = h2 * lax.rsqrt(ms + EPS) * fw_ref[...]


def _ffn(h, norm_w, wu, wd, final_w):
    m = h.shape[0]
    return pl.pallas_call(
        _ffn_kernel,
        grid=(m // FFN_TM, D_FF // FFN_TF),
        in_specs=[
            pl.BlockSpec((FFN_TM, D_MODEL), lambda i, f: (i, 0)),
            pl.BlockSpec((1, D_MODEL), lambda i, f: (0, 0)),
            pl.BlockSpec((D_MODEL, FFN_TF), lambda i, f: (0, f)),
            pl.BlockSpec((FFN_TF, D_MODEL), lambda i, f: (f, 0)),
            pl.BlockSpec((1, D_MODEL), lambda i, f: (0, 0)),
        ],
        out_specs=pl.BlockSpec((FFN_TM, D_MODEL), lambda i, f: (i, 0)),
        out_shape=jax.ShapeDtypeStruct((m, D_MODEL), F32),
        scratch_shapes=[
            pltpu.VMEM((FFN_TM, D_MODEL), BF16),
            pltpu.VMEM((FFN_TM, D_MODEL), F32),
        ],
        compiler_params=pltpu.CompilerParams(
            dimension_semantics=("arbitrary", "arbitrary"), vmem_limit_bytes=VMEM_LIMIT),
        name="ffn",
    )(h, norm_w, wu, wd, final_w)


def _layer(h2, batch, seq, norm_mix_w, w_in, w_alpha_up, b_alpha, gla_norm_w, gmlp_ln_w,
           gmlp_ln_b, w_spatial, b_spatial, b_gate, w_branch, w_out, norm_mlp_w, w_ff_up,
           w_ff_down, final_w):
    w_in_t = w_in.T
    xn, q, glr = _qnorm(h2, norm_mix_w.reshape(1, D_MODEL), w_in_t)
    p, wu_b, wd_b = _in_proj(xn, w_in_t, b_gate.reshape(1, 2 * D_MODEL),
                             gmlp_ln_w.reshape(1, GMLP_DG), gmlp_ln_b.reshape(1, GMLP_DG),
                             w_ff_up, w_ff_down)
    o_gla, wb0_b, wb1_b, wo_b = _gla(q, p, glr, w_alpha_up.astype(BF16),
                                     b_alpha.reshape(1, QK_W), gla_norm_w.reshape(1, GLA_DV),
                                     w_branch, w_out, batch, seq)
    b_sp = jnp.broadcast_to(b_spatial[:, :, None], (GMLP_GROUPS, GMLP_BLOCK, GMLP_DG))
    h_mid = _merge(o_gla, p, h2, w_spatial, b_sp, wb0_b, wb1_b, wo_b)
    return _ffn(h_mid, norm_mlp_w.reshape(1, D_MODEL), wu_b, wd_b, final_w)


def kernel(x, norm_mix_w, w_in, w_alpha_up, b_alpha, gla_norm_w, gmlp_ln_w, gmlp_ln_b,
           w_spatial, b_spatial, b_gate, w_branch, w_out, norm_mlp_w, w_ff_up, w_ff_down,
           norm_final_w):
    batch, seq, d = x.shape
    depth = w_in.shape[0]
    assert d == D_MODEL and depth == 1 and seq % GLA_TB == 0
    assert w_in.shape[2] == D_IN
    h2 = x.reshape(batch * seq, d)
    y = _layer(h2, batch, seq, norm_mix_w[0], w_in[0], w_alpha_up[0], b_alpha[0],
               gla_norm_w[0], gmlp_ln_w[0], gmlp_ln_b[0], w_spatial[0], b_spatial[0],
               b_gate[0], w_branch[0], w_out[0], norm_mlp_w[0], w_ff_up[0], w_ff_down[0],
               norm_final_w.reshape(1, D_MODEL))
    return y.reshape(batch, seq, d)
```

```python
import jax
import jax.numpy as jnp
from jax import lax
from jax.experimental import pallas as pl
from jax.experimental.pallas import tpu as pltpu

D_MODEL = 2048
CHUNK = 64
GLA_HEADS = 4
GLA_DK = D_MODEL // 8
GLA_DV = D_MODEL // 4
GLA_GATE_RANK = 16
GLA_TAU = 16.0
GMLP_BLOCK = 128
GMLP_GROUPS = 8
GMLP_DG = D_MODEL // GMLP_GROUPS
D_FF = 4 * D_MODEL
EPS = 1e-6
QK_W = GLA_HEADS * GLA_DK
V_W = GLA_HEADS * GLA_DV
D_IN = 2 * QK_W + 2 * V_W + GLA_GATE_RANK + 4 * D_MODEL

F32 = jnp.float32
BF16 = jnp.bfloat16

LANES = 128

WCOL_Q = 0
WCOL_K = QK_W
WCOL_V = 2 * QK_W
WCOL_R = WCOL_V + V_W
WCOL_GLR = WCOL_R + V_W
WCOL_GU = WCOL_GLR + GLA_GATE_RANK

COL_V = 0
COL_R = COL_V + V_W
COL_GU = COL_R + V_W
COL_GV = COL_GU + D_MODEL
COL_GATE = COL_GV + D_MODEL
COL_K = COL_GATE + 2 * D_MODEL
P_WIDTH = COL_K + QK_W

V7X_VMEM_BYTES = 64 * 1024 * 1024
VMEM_LIMIT = V7X_VMEM_BYTES - 8 * 1024 * 1024

QN_TM = 1024
IN_TM = 2048
IN_TN = 1024
IN_RC = 256
CAST_RC = 256
GLA_TB = 256
GLA_UNROLL = 4
MERGE_TM = 256
FFN_TM = 512
FFN_TF = 1024


def _sigmoid(x):
    return 0.5 * (1.0 + jnp.tanh(0.5 * x))


def _gelu(x):
    return jax.nn.gelu(x, approximate=True)


def _cast_transposed(wt_ref, wb_ref, n_rows):
    for rc in range(n_rows // CAST_RC):
        rows = pl.ds(rc * CAST_RC, CAST_RC)
        wb_ref[:, rc * CAST_RC:(rc + 1) * CAST_RC] = wt_ref[rows, :].T.astype(BF16)


def _qnorm_kernel(x_ref, nw_ref, wq_ref, wg_ref, xn_ref, q_ref, glr_ref, wqb_ref, wgb_ref):
    @pl.when(pl.program_id(0) == 0)
    def _():
        _cast_transposed(wq_ref, wqb_ref, QK_W)
        wgb_ref[...] = wg_ref[...].T.astype(BF16)

    for c in range(QN_TM // IN_RC):
        rows = pl.ds(c * IN_RC, IN_RC)
        x = x_ref[rows, :]
        ms = jnp.mean(x * x, axis=-1, keepdims=True)
        xn = (x * lax.rsqrt(ms + EPS) * nw_ref[...]).astype(BF16)
        xn_ref[rows, :] = xn
        q = jnp.dot(xn, wqb_ref[...], preferred_element_type=F32)
        q_ref[rows, :] = (q * (GLA_DK ** -0.5)).astype(BF16)
        glr_ref[rows, :] = jnp.dot(xn, wgb_ref[...], preferred_element_type=F32)


def _qnorm(x2, norm_w, w_in_t):
    m = x2.shape[0]
    single = pl.Buffered(1)
    return pl.pallas_call(
        _qnorm_kernel,
        grid=(m // QN_TM,),
        in_specs=[
            pl.BlockSpec((QN_TM, D_MODEL), lambda i: (i, 0)),
            pl.BlockSpec((1, D_MODEL), lambda i: (0, 0)),
            pl.BlockSpec((QK_W, D_MODEL), lambda i: (WCOL_Q // QK_W, 0), pipeline_mode=single),
            pl.BlockSpec((LANES, D_MODEL), lambda i: (WCOL_GLR // LANES, 0), pipeline_mode=single),
        ],
        out_specs=[
            pl.BlockSpec((QN_TM, D_MODEL), lambda i: (i, 0)),
            pl.BlockSpec((QN_TM, QK_W), lambda i: (i, 0)),
            pl.BlockSpec((QN_TM, LANES), lambda i: (i, 0)),
        ],
        out_shape=[
            jax.ShapeDtypeStruct((m, D_MODEL), BF16),
            jax.ShapeDtypeStruct((m, QK_W), BF16),
            jax.ShapeDtypeStruct((m, LANES), F32),
        ],
        scratch_shapes=[pltpu.VMEM((D_MODEL, QK_W), BF16), pltpu.VMEM((D_MODEL, LANES), BF16)],
        compiler_params=pltpu.CompilerParams(
            dimension_semantics=("arbitrary",), vmem_limit_bytes=VMEM_LIMIT),
        name="qnorm",
    )(x2, norm_w, w_in_t, w_in_t)


_T_R, _T_GU, _T_GV, _T_GATE, _T_K = (c // IN_TN for c in (COL_R, COL_GU, COL_GV, COL_GATE, COL_K))
_N_T = P_WIDTH // IN_TN


def _in_proj_kernel(xn_ref, w_ref, bg_ref, lnw_ref, lnb_ref, p_ref, wb_ref):
    t = pl.program_id(0)
    i = pl.program_id(1)

    @pl.when(i == 0)
    def _():
        _cast_transposed(w_ref, wb_ref, IN_TN)

    def run(epilogue):
        for c in range(IN_TM // IN_RC):
            rows = pl.ds(c * IN_RC, IN_RC)
            acc = jnp.dot(xn_ref[rows, :], wb_ref[...], preferred_element_type=F32)
            p_ref[rows, :] = epilogue(acc).astype(BF16)

    def epi_ln(acc):
        outs = []
        for g in range(IN_TN // GMLP_DG):
            cols = slice(g * GMLP_DG, (g + 1) * GMLP_DG)
            v = _gelu(acc[:, cols].astype(BF16)).astype(F32)
            mu = jnp.mean(v, axis=-1, keepdims=True)
            vc = v - mu
            var = jnp.mean(vc * vc, axis=-1, keepdims=True)
            outs.append(vc * lax.rsqrt(var + EPS) * lnw_ref[...] + lnb_ref[...])
        return jnp.concatenate(outs, axis=-1)

    @pl.when(jnp.logical_or(t < _T_R, t >= _T_K))
    def _():
        run(lambda a: a)

    @pl.when(jnp.logical_and(t >= _T_R, t < _T_GU))
    def _():
        run(lambda a: a.astype(BF16) * _sigmoid(a.astype(BF16)))

    @pl.when(jnp.logical_and(t >= _T_GU, t < _T_GV))
    def _():
        run(lambda a: _gelu(a.astype(BF16)))

    @pl.when(jnp.logical_and(t >= _T_GV, t < _T_GATE))
    def _():
        run(epi_ln)

    @pl.when(jnp.logical_and(t >= _T_GATE, t < _T_K))
    def _():
        run(lambda a: _sigmoid((a + bg_ref[...]).astype(BF16)))


def _in_proj(xn, w_in_t, b_gate, ln_w, ln_b):
    m = xn.shape[0]
    n_i = m // IN_TM
    n_gate_blocks = (2 * D_MODEL) // IN_TN

    def w_rows(t, i):
        col = jnp.where(t < _T_GU, WCOL_V + t * IN_TN,
                        jnp.where(t < _T_K, WCOL_GU + (t - _T_GU) * IN_TN, WCOL_K))
        return (pl.multiple_of(col, GLA_GATE_RANK), 0)

    return pl.pallas_call(
        _in_proj_kernel,
        grid=(_N_T, n_i),
        in_specs=[
            pl.BlockSpec((IN_TM, D_MODEL), lambda t, i: (i, 0)),
            pl.BlockSpec((pl.Element(IN_TN), pl.Element(D_MODEL)), w_rows),
            pl.BlockSpec((1, IN_TN),
                         lambda t, i: (0, jnp.clip(t - _T_GATE, 0, n_gate_blocks - 1))),
            pl.BlockSpec((1, GMLP_DG), lambda t, i: (0, 0)),
            pl.BlockSpec((1, GMLP_DG), lambda t, i: (0, 0)),
        ],
        out_specs=pl.BlockSpec((IN_TM, IN_TN), lambda t, i: (i, t)),
        out_shape=jax.ShapeDtypeStruct((m, P_WIDTH), BF16),
        scratch_shapes=[pltpu.VMEM((D_MODEL, IN_TN), BF16)],
        compiler_params=pltpu.CompilerParams(
            dimension_semantics=("arbitrary", "arbitrary"), vmem_limit_bytes=VMEM_LIMIT),
        name="in_proj",
    )(xn, w_in_t, b_gate, ln_w, ln_b)


def _gla_kernel(q_ref, k_ref, v_ref, rs_ref, glr_ref, wup_ref, ba_ref, gnw_ref,
                wb0_ref, wb1_ref, wo_ref, wu_ref, wd_ref,
                o_ref, wb0b_ref, wb1b_ref, wob_ref, wub_ref, wdb_ref, state_ref, z_ref):
    t = pl.program_id(1)

    @pl.when(t == 0)
    def _():
        state_ref[...] = jnp.zeros_like(state_ref)

    glr = glr_ref[:, :GLA_GATE_RANK].astype(BF16)
    z_ref[...] = jnp.dot(glr, wup_ref[...], preferred_element_type=F32) + ba_ref[...]

    row = lax.broadcasted_iota(jnp.int32, (CHUNK, CHUNK), 0)
    col = lax.broadcasted_iota(jnp.int32, (CHUNK, CHUNK), 1)
    tri = (row >= col).astype(BF16)

    n_chunks = GLA_TB // CHUNK
    w_rows = wb0_ref.shape[0] // n_chunks
    wu_rows = wu_ref.shape[0] // n_chunks
    wd_rows = wd_ref.shape[0] // n_chunks

    def chunk_body(c, carry):
        for src, dst, n in ((wb0_ref, wb0b_ref, w_rows), (wb1_ref, wb1b_ref, w_rows),
                            (wo_ref, wob_ref, w_rows), (wu_ref, wub_ref, wu_rows),
                            (wd_ref, wdb_ref, wd_rows)):
            wr = pl.ds(pl.multiple_of(c * n, n), n)
            dst[wr, :] = src[wr, :].astype(BF16)

        rows = pl.ds(pl.multiple_of(c * CHUNK, CHUNK), CHUNK)
        z = z_ref[rows, :]
        la = (jnp.minimum(z, 0.0) - jnp.log1p(jnp.exp(-jnp.abs(z)))) * (1.0 / GLA_TAU)
        la_hi = la.astype(BF16)
        la_lo = (la - la_hi.astype(F32)).astype(BF16)
        lcum = (jnp.dot(tri, la_hi, preferred_element_type=F32)
                + jnp.dot(tri, la_lo, preferred_element_type=F32))
        l_end = lcum[CHUNK - 1:CHUNK, :]
        k_dec = (k_ref[rows, :].astype(F32) * jnp.exp(l_end - lcum)).astype(BF16)
        decay = jnp.exp(l_end)
        q = q_ref[rows, :]
        v = v_ref[rows, :]
        rs = rs_ref[rows, :]
        for h in range(GLA_HEADS):
            kc = slice(h * GLA_DK, (h + 1) * GLA_DK)
            vc = slice(h * GLA_DV, (h + 1) * GLA_DV)
            kv_t = lax.dot_general(v[:, vc], k_dec[:, kc], (((0,), (0,)), ((), ())),
                                   preferred_element_type=F32)
            s = state_ref[h] * decay[:, kc] + kv_t
            state_ref[h] = s
            o = lax.dot_general(q[:, kc], s.astype(BF16), (((1,), (1,)), ((), ())),
                                preferred_element_type=F32)
            ms = jnp.mean(o * o, axis=-1, keepdims=True)
            o = o * lax.rsqrt(ms + EPS) * gnw_ref[...] * rs[:, vc].astype(F32)
            o_ref[rows, vc] = o.astype(BF16)
        return carry

    lax.fori_loop(0, n_chunks, chunk_body, 0, unroll=GLA_UNROLL)


def _gla(q, p, glr, wup, b_alpha, gnw, w_branch, w_out, w_ff_up, w_ff_down, batch, seq):
    m = p.shape[0]
    nt = seq // GLA_TB
    n_steps = batch * nt
    rowblk = lambda b, t: b * nt + t
    w_spec = pl.BlockSpec((D_MODEL // n_steps, D_MODEL), lambda b, t: (rowblk(b, t), 0))
    wbr_spec = lambda n: pl.BlockSpec((None, D_MODEL // n_steps, D_MODEL),
                                      lambda b, t: (n, rowblk(b, t), 0))
    wu_spec = pl.BlockSpec((D_MODEL // n_steps, D_FF), lambda b, t: (rowblk(b, t), 0))
    wd_spec = pl.BlockSpec((D_FF // n_steps, D_MODEL), lambda b, t: (rowblk(b, t), 0))
    w_shape = jax.ShapeDtypeStruct((D_MODEL, D_MODEL), BF16)
    return pl.pallas_call(
        _gla_kernel,
        grid=(batch, nt),
        in_specs=[
            pl.BlockSpec((GLA_TB, QK_W), lambda b, t: (rowblk(b, t), 0)),
            pl.BlockSpec((GLA_TB, QK_W), lambda b, t: (rowblk(b, t), COL_K // QK_W)),
            pl.BlockSpec((GLA_TB, V_W), lambda b, t: (rowblk(b, t), COL_V // V_W)),
            pl.BlockSpec((GLA_TB, V_W), lambda b, t: (rowblk(b, t), COL_R // V_W)),
            pl.BlockSpec((GLA_TB, LANES), lambda b, t: (rowblk(b, t), 0)),
            pl.BlockSpec((GLA_GATE_RANK, QK_W), lambda b, t: (0, 0)),
            pl.BlockSpec((1, QK_W), lambda b, t: (0, 0)),
            pl.BlockSpec((1, GLA_DV), lambda b, t: (0, 0)),
            wbr_spec(0), wbr_spec(1), w_spec, wu_spec, wd_spec,
        ],
        out_specs=[pl.BlockSpec((GLA_TB, V_W), lambda b, t: (rowblk(b, t), 0)),
                   w_spec, w_spec, w_spec, wu_spec, wd_spec],
        out_shape=[jax.ShapeDtypeStruct((m, V_W), BF16), w_shape, w_shape, w_shape,
                   jax.ShapeDtypeStruct((D_MODEL, D_FF), BF16),
                   jax.ShapeDtypeStruct((D_FF, D_MODEL), BF16)],
        scratch_shapes=[
            pltpu.VMEM((GLA_HEADS, GLA_DV, GLA_DK), F32),
            pltpu.VMEM((GLA_TB, QK_W), F32),
        ],
        compiler_params=pltpu.CompilerParams(
            dimension_semantics=("arbitrary", "arbitrary"), vmem_limit_bytes=VMEM_LIMIT),
        name="gla",
    )(q, p, p, p, glr, wup, b_alpha, gnw, w_branch, w_branch, w_out, w_ff_up, w_ff_down)


def _merge_kernel(og_ref, gu_ref, gv_ref, gate0_ref, gate1_ref, x_ref, ws_ref, bs_ref,
                  wb0_ref, wb1_ref, wo_ref, h_ref, gm_ref):
    pos_t = lax.broadcasted_iota(jnp.int32, (GMLP_BLOCK, GMLP_BLOCK), 0) // CHUNK
    pos_s = lax.broadcasted_iota(jnp.int32, (GMLP_BLOCK, GMLP_BLOCK), 1) // CHUNK
    causal = pos_t >= pos_s
    for g in range(GMLP_GROUPS):
        w = jnp.where(causal, ws_ref[g], 0.0).astype(BF16)
        cols = slice(g * GMLP_DG, (g + 1) * GMLP_DG)
        for blk in range(MERGE_TM // GMLP_BLOCK):
            rows = pl.ds(blk * GMLP_BLOCK, GMLP_BLOCK)
            mixed = jnp.dot(w, gv_ref[rows, cols], preferred_element_type=F32) + bs_ref[g]
            gm_ref[rows, cols] = (gu_ref[rows, cols].astype(F32) * mixed).astype(BF16)

    d0 = jnp.dot(og_ref[...], wb0_ref[...], preferred_element_type=F32)
    d1 = jnp.dot(gm_ref[...], wb1_ref[...], preferred_element_type=F32)
    mixed = (gate0_ref[...].astype(F32) * d0 + gate1_ref[...].astype(F32) * d1).astype(BF16)
    h_ref[...] = x_ref[...] + jnp.dot(mixed, wo_ref[...], preferred_element_type=F32)


def _merge(o_gla, p, x2, w_spatial, b_sp, wb0, wb1, wo):
    m = x2.shape[0]
    const2 = lambda i: (0, 0)
    const3 = lambda i: (0, 0, 0)
    single = pl.Buffered(1)
    return pl.pallas_call(
        _merge_kernel,
        grid=(m // MERGE_TM,),
        in_specs=[
            pl.BlockSpec((MERGE_TM, V_W), lambda i: (i, 0)),
            pl.BlockSpec((MERGE_TM, D_MODEL), lambda i: (i, COL_GU // D_MODEL)),
            pl.BlockSpec((MERGE_TM, D_MODEL), lambda i: (i, COL_GV // D_MODEL)),
            pl.BlockSpec((MERGE_TM, D_MODEL), lambda i: (i, COL_GATE // D_MODEL)),
            pl.BlockSpec((MERGE_TM, D_MODEL), lambda i: (i, COL_GATE // D_MODEL + 1)),
            pl.BlockSpec((MERGE_TM, D_MODEL), lambda i: (i, 0)),
            pl.BlockSpec((GMLP_GROUPS, GMLP_BLOCK, GMLP_BLOCK), const3, pipeline_mode=single),
            pl.BlockSpec((GMLP_GROUPS, GMLP_BLOCK, GMLP_DG), const3, pipeline_mode=single),
            pl.BlockSpec((V_W, D_MODEL), const2, pipeline_mode=single),
            pl.BlockSpec((V_W, D_MODEL), const2, pipeline_mode=single),
            pl.BlockSpec((D_MODEL, D_MODEL), const2, pipeline_mode=single),
        ],
        out_specs=pl.BlockSpec((MERGE_TM, D_MODEL), lambda i: (i, 0)),
        out_shape=jax.ShapeDtypeStruct((m, D_MODEL), F32),
        scratch_shapes=[pltpu.VMEM((MERGE_TM, D_MODEL), BF16)],
        compiler_params=pltpu.CompilerParams(
            dimension_semantics=("arbitrary",), vmem_limit_bytes=VMEM_LIMIT),
        name="merge",
    )(o_gla, p, p, p, p, x2, w_spatial, b_sp, wb0, wb1, wo)


def _ffn_kernel(h_ref, nw_ref, wu_ref, wd_ref, fw_ref, y_ref, hn_ref, acc_ref):
    f = pl.program_id(1)

    @pl.when(f == 0)
    def _():
        h = h_ref[...]
        ms = jnp.mean(h * h, axis=-1, keepdims=True)
        hn_ref[...] = (h * lax.rsqrt(ms + EPS) * nw_ref[...]).astype(BF16)
        acc_ref[...] = h

    up = jnp.dot(hn_ref[...], wu_ref[...], preferred_element_type=F32)
    up = jnp.square(jnp.maximum(up, 0.0)).astype(BF16)
    acc_ref[...] += jnp.dot(up, wd_ref[...], preferred_element_type=F32)

    @pl.when(f == pl.num_programs(1) - 1)
    def _():
        h2 = acc_ref[...]
        ms = jnp.mean(h2 * h2, axis=-1, keepdims=True)
        y_ref[...] = h2 * lax.rsqrt(ms + EPS) * fw_ref[...]


def _ffn(h, norm_w, wu, wd, final_w):
    m = h.shape[0]
    return pl.pallas_call(
        _ffn_kernel,
        grid=(m // FFN_TM, D_FF // FFN_TF),
        in_specs=[
            pl.BlockSpec((FFN_TM, D_MODEL), lambda i, f: (i, 0)),
            pl.BlockSpec((1, D_MODEL), lambda i, f: (0, 0)),
            pl.BlockSpec((D_MODEL, FFN_TF), lambda i, f: (0, f)),
            pl.BlockSpec((FFN_TF, D_MODEL), lambda i, f: (f, 0)),
            pl.BlockSpec((1, D_MODEL), lambda i, f: (0, 0)),
        ],
        out_specs=pl.BlockSpec((FFN_TM, D_MODEL), lambda i, f: (i, 0)),
        out_shape=jax.ShapeDtypeStruct((m, D_MODEL), F32),
        scratch_shapes=[
            pltpu.VMEM((FFN_TM, D_MODEL), BF16),
            pltpu.VMEM((FFN_TM, D_MODEL), F32),
        ],
        compiler_params=pltpu.CompilerParams(
            dimension_semantics=("arbitrary", "arbitrary"), vmem_limit_bytes=VMEM_LIMIT),
        name="ffn",
    )(h, norm_w, wu, wd, final_w)


def _layer(h2, batch, seq, norm_mix_w, w_in, w_alpha_up, b_alpha, gla_norm_w, gmlp_ln_w,
           gmlp_ln_b, w_spatial, b_spatial, b_gate, w_branch, w_out, norm_mlp_w, w_ff_up,
           w_ff_down, final_w):
    w_in_t = w_in.T
    xn, q, glr = _qnorm(h2, norm_mix_w.reshape(1, D_MODEL), w_in_t)
    p = _in_proj(xn, w_in_t, b_gate.reshape(1, 2 * D_MODEL), gmlp_ln_w.reshape(1, GMLP_DG),
                 gmlp_ln_b.reshape(1, GMLP_DG))
    o_gla, wb0_b, wb1_b, wo_b, wu_b, wd_b = _gla(
        q, p, glr, w_alpha_up.astype(BF16), b_alpha.reshape(1, QK_W),
        gla_norm_w.reshape(1, GLA_DV), w_branch, w_out, w_ff_up, w_ff_down, batch, seq)
    b_sp = jnp.broadcast_to(b_spatial[:, :, None], (GMLP_GROUPS, GMLP_BLOCK, GMLP_DG))
    h_mid = _merge(o_gla, p, h2, w_spatial, b_sp, wb0_b, wb1_b, wo_b)
    return _ffn(h_mid, norm_mlp_w.reshape(1, D_MODEL), wu_b, wd_b, final_w)


def kernel(x, norm_mix_w, w_in, w_alpha_up, b_alpha, gla_norm_w, gmlp_ln_w, gmlp_ln_b,
           w_spatial, b_spatial, b_gate, w_branch, w_out, norm_mlp_w, w_ff_up, w_ff_down,
           norm_final_w):
    batch, seq, d = x.shape
    depth = w_in.shape[0]
    assert d == D_MODEL and depth == 1 and seq % GLA_TB == 0
    assert w_in.shape[2] == D_IN
    h2 = x.reshape(batch * seq, d)
    y = _layer(h2, batch, seq, norm_mix_w[0], w_in[0], w_alpha_up[0], b_alpha[0],
               gla_norm_w[0], gmlp_ln_w[0], gmlp_ln_b[0], w_spatial[0], b_spatial[0],
               b_gate[0], w_branch[0], w_out[0], norm_mlp_w[0], w_ff_up[0], w_ff_down[0],
               norm_final_w.reshape(1, D_MODEL))
    return y.reshape(batch, seq, d)
```

```python
import math

import jax
import jax.numpy as jnp
from jax import lax
from jax.experimental import pallas as pl
from jax.experimental.pallas import tpu as pltpu

D_MODEL = 2048
CHUNK = 64
GLA_HEADS = 4
GLA_DK = D_MODEL // 8
GLA_DV = D_MODEL // 4
GLA_GATE_RANK = 16
GLA_TAU = 16.0
GMLP_BLOCK = 128
GMLP_GROUPS = 8
GMLP_DG = D_MODEL // GMLP_GROUPS
D_FF = 4 * D_MODEL
EPS = 1e-6
QK_W = GLA_HEADS * GLA_DK
V_W = GLA_HEADS * GLA_DV
D_IN = 2 * QK_W + 2 * V_W + GLA_GATE_RANK + 4 * D_MODEL

F32 = jnp.float32
BF16 = jnp.bfloat16

LANES = 128

WCOL_Q = 0
WCOL_K = QK_W
WCOL_V = 2 * QK_W
WCOL_R = WCOL_V + V_W
WCOL_GLR = WCOL_R + V_W
WCOL_GU = WCOL_GLR + GLA_GATE_RANK

COL_V = 0
COL_R = COL_V + V_W
COL_GU = COL_R + V_W
COL_GV = COL_GU + D_MODEL
COL_GATE = COL_GV + D_MODEL
COL_K = COL_GATE + 2 * D_MODEL
P_WIDTH = COL_K + QK_W

V7X_VMEM_BYTES = 64 * 1024 * 1024
VMEM_LIMIT = V7X_VMEM_BYTES - 8 * 1024 * 1024

QN_TM = 1024
IN_TM = 2048
IN_TN = 1024
IN_RC = 256
IN_UNROLL = 2
CAST_RC = 256
GLA_TB = 256
GLA_UNROLL = 4
MERGE_TM = 256
FFN_TM = 512
FFN_TF = 1024


def _sigmoid(x):
    return 0.5 * (1.0 + jnp.tanh(0.5 * x))


def _gelu_bf16(x):
    inner = math.sqrt(2.0 / math.pi) * (x + 0.044715 * (x * x * x))
    return (0.5 * x.astype(BF16)) * (1.0 + jnp.tanh(inner.astype(BF16)))


def _cast_transposed(wt_ref, wb_ref, n_rows):
    for rc in range(n_rows // CAST_RC):
        rows = pl.ds(rc * CAST_RC, CAST_RC)
        wb_ref[:, rc * CAST_RC:(rc + 1) * CAST_RC] = wt_ref[rows, :].T.astype(BF16)


def _qnorm_kernel(x_ref, nw_ref, wq_ref, wg_ref, xn_ref, q_ref, glr_ref, wqb_ref, wgb_ref):
    @pl.when(pl.program_id(0) == 0)
    def _():
        _cast_transposed(wq_ref, wqb_ref, QK_W)
        wgb_ref[...] = wg_ref[...].T.astype(BF16)

    for c in range(QN_TM // IN_RC):
        rows = pl.ds(c * IN_RC, IN_RC)
        x = x_ref[rows, :]
        ms = jnp.mean(x * x, axis=-1, keepdims=True)
        xn = (x * lax.rsqrt(ms + EPS) * nw_ref[...]).astype(BF16)
        xn_ref[rows, :] = xn
        q = jnp.dot(xn, wqb_ref[...], preferred_element_type=F32)
        q_ref[rows, :] = (q * (GLA_DK ** -0.5)).astype(BF16)
        glr_ref[rows, :] = jnp.dot(xn, wgb_ref[...], preferred_element_type=F32)


def _qnorm(x2, norm_w, w_in_t):
    m = x2.shape[0]
    single = pl.Buffered(1)
    return pl.pallas_call(
        _qnorm_kernel,
        grid=(m // QN_TM,),
        in_specs=[
            pl.BlockSpec((QN_TM, D_MODEL), lambda i: (i, 0)),
            pl.BlockSpec((1, D_MODEL), lambda i: (0, 0)),
            pl.BlockSpec((QK_W, D_MODEL), lambda i: (WCOL_Q // QK_W, 0), pipeline_mode=single),
            pl.BlockSpec((LANES, D_MODEL), lambda i: (WCOL_GLR // LANES, 0), pipeline_mode=single),
        ],
        out_specs=[
            pl.BlockSpec((QN_TM, D_MODEL), lambda i: (i, 0)),
            pl.BlockSpec((QN_TM, QK_W), lambda i: (i, 0)),
            pl.BlockSpec((QN_TM, LANES), lambda i: (i, 0)),
        ],
        out_shape=[
            jax.ShapeDtypeStruct((m, D_MODEL), BF16),
            jax.ShapeDtypeStruct((m, QK_W), BF16),
            jax.ShapeDtypeStruct((m, LANES), F32),
        ],
        scratch_shapes=[pltpu.VMEM((D_MODEL, QK_W), BF16), pltpu.VMEM((D_MODEL, LANES), BF16)],
        compiler_params=pltpu.CompilerParams(
            dimension_semantics=("arbitrary",), vmem_limit_bytes=VMEM_LIMIT),
        name="qnorm",
    )(x2, norm_w, w_in_t, w_in_t)


_T_R, _T_GU, _T_GV, _T_GATE, _T_K = (c // IN_TN for c in (COL_R, COL_GU, COL_GV, COL_GATE, COL_K))
_N_T = P_WIDTH // IN_TN


def _in_proj_kernel(xn_ref, w_ref, bg_ref, lnw_ref, lnb_ref, p_ref, wb_ref):
    t = pl.program_id(0)
    i = pl.program_id(1)

    @pl.when(i == 0)
    def _():
        _cast_transposed(w_ref, wb_ref, IN_TN)

    def run(epilogue):
        def chunk_body(c, carry):
            rows = pl.ds(pl.multiple_of(c * IN_RC, IN_RC), IN_RC)
            acc = jnp.dot(xn_ref[rows, :], wb_ref[...], preferred_element_type=F32)
            p_ref[rows, :] = epilogue(acc).astype(BF16)
            return carry

        lax.fori_loop(0, IN_TM // IN_RC, chunk_body, 0, unroll=IN_UNROLL)

    def epi_ln(acc):
        outs = []
        for g in range(IN_TN // GMLP_DG):
            cols = slice(g * GMLP_DG, (g + 1) * GMLP_DG)
            v = _gelu_bf16(acc[:, cols]).astype(F32)
            mu = jnp.mean(v, axis=-1, keepdims=True)
            vc = v - mu
            var = jnp.mean(vc * vc, axis=-1, keepdims=True)
            outs.append(vc * lax.rsqrt(var + EPS) * lnw_ref[...] + lnb_ref[...])
        return jnp.concatenate(outs, axis=-1)

    @pl.when(jnp.logical_or(t < _T_R, t >= _T_K))
    def _():
        run(lambda a: a)

    @pl.when(jnp.logical_and(t >= _T_R, t < _T_GU))
    def _():
        run(lambda a: a.astype(BF16) * _sigmoid(a.astype(BF16)))

    @pl.when(jnp.logical_and(t >= _T_GU, t < _T_GV))
    def _():
        run(_gelu_bf16)

    @pl.when(jnp.logical_and(t >= _T_GV, t < _T_GATE))
    def _():
        run(epi_ln)

    @pl.when(jnp.logical_and(t >= _T_GATE, t < _T_K))
    def _():
        run(lambda a: _sigmoid((a + bg_ref[...]).astype(BF16)))


def _in_proj(xn, w_in_t, b_gate, ln_w, ln_b):
    m = xn.shape[0]
    n_i = m // IN_TM
    n_gate_blocks = (2 * D_MODEL) // IN_TN

    def w_rows(t, i):
        col = jnp.where(t < _T_GU, WCOL_V + t * IN_TN,
                        jnp.where(t < _T_K, WCOL_GU + (t - _T_GU) * IN_TN, WCOL_K))
        return (pl.multiple_of(col, GLA_GATE_RANK), 0)

    return pl.pallas_call(
        _in_proj_kernel,
        grid=(_N_T, n_i),
        in_specs=[
            pl.BlockSpec((IN_TM, D_MODEL), lambda t, i: (i, 0)),
            pl.BlockSpec((pl.Element(IN_TN), pl.Element(D_MODEL)), w_rows),
            pl.BlockSpec((1, IN_TN),
                         lambda t, i: (0, jnp.clip(t - _T_GATE, 0, n_gate_blocks - 1))),
            pl.BlockSpec((1, GMLP_DG), lambda t, i: (0, 0)),
            pl.BlockSpec((1, GMLP_DG), lambda t, i: (0, 0)),
        ],
        out_specs=pl.BlockSpec((IN_TM, IN_TN), lambda t, i: (i, t)),
        out_shape=jax.ShapeDtypeStruct((m, P_WIDTH), BF16),
        scratch_shapes=[pltpu.VMEM((D_MODEL, IN_TN), BF16)],
        compiler_params=pltpu.CompilerParams(
            dimension_semantics=("arbitrary", "arbitrary"), vmem_limit_bytes=VMEM_LIMIT),
        name="in_proj",
    )(xn, w_in_t, b_gate, ln_w, ln_b)


def _gla_kernel(q_ref, k_ref, v_ref, rs_ref, glr_ref, wup_ref, ba_ref, gnw_ref,
                wb0_ref, wb1_ref, wo_ref, wu_ref, wd_ref,
                o_ref, wb0b_ref, wb1b_ref, wob_ref, wub_ref, wdb_ref, state_ref, z_ref):
    t = pl.program_id(1)

    @pl.when(t == 0)
    def _():
        state_ref[...] = jnp.zeros_like(state_ref)

    glr = glr_ref[:, :GLA_GATE_RANK].astype(BF16)
    z_ref[...] = jnp.dot(glr, wup_ref[...], preferred_element_type=F32) + ba_ref[...]

    row = lax.broadcasted_iota(jnp.int32, (CHUNK, CHUNK), 0)
    col = lax.broadcasted_iota(jnp.int32, (CHUNK, CHUNK), 1)
    tri = (row >= col).astype(BF16)

    n_chunks = GLA_TB // CHUNK
    w_rows = wb0_ref.shape[0] // n_chunks
    wu_rows = wu_ref.shape[0] // n_chunks
    wd_rows = wd_ref.shape[0] // n_chunks

    def chunk_body(c, carry):
        for src, dst, n in ((wb0_ref, wb0b_ref, w_rows), (wb1_ref, wb1b_ref, w_rows),
                            (wo_ref, wob_ref, w_rows), (wu_ref, wub_ref, wu_rows),
                            (wd_ref, wdb_ref, wd_rows)):
            wr = pl.ds(pl.multiple_of(c * n, n), n)
            dst[wr, :] = src[wr, :].astype(BF16)

        rows = pl.ds(pl.multiple_of(c * CHUNK, CHUNK), CHUNK)
        z = z_ref[rows, :]
        la = (jnp.minimum(z, 0.0) - jnp.log1p(jnp.exp(-jnp.abs(z)))) * (1.0 / GLA_TAU)
        la_hi = la.astype(BF16)
        la_lo = (la - la_hi.astype(F32)).astype(BF16)
        lcum = (jnp.dot(tri, la_hi, preferred_element_type=F32)
                + jnp.dot(tri, la_lo, preferred_element_type=F32))
        l_end = lcum[CHUNK - 1:CHUNK, :]
        k_dec = (k_ref[rows, :].astype(F32) * jnp.exp(l_end - lcum)).astype(BF16)
        decay = jnp.exp(l_end)
        q = q_ref[rows, :]
        v = v_ref[rows, :]
        rs = rs_ref[rows, :]
        for h in range(GLA_HEADS):
            kc = slice(h * GLA_DK, (h + 1) * GLA_DK)
            vc = slice(h * GLA_DV, (h + 1) * GLA_DV)
            kv_t = lax.dot_general(v[:, vc], k_dec[:, kc], (((0,), (0,)), ((), ())),
                                   preferred_element_type=F32)
            s = state_ref[h] * decay[:, kc] + kv_t
            state_ref[h] = s
            o = lax.dot_general(q[:, kc], s.astype(BF16), (((1,), (1,)), ((), ())),
                                preferred_element_type=F32)
            ms = jnp.mean(o * o, axis=-1, keepdims=True)
            o = o * lax.rsqrt(ms + EPS) * gnw_ref[...] * rs[:, vc].astype(F32)
            o_ref[rows, vc] = o.astype(BF16)
        return carry

    lax.fori_loop(0, n_chunks, chunk_body, 0, unroll=GLA_UNROLL)


def _gla(q, p, glr, wup, b_alpha, gnw, w_branch, w_out, w_ff_up, w_ff_down, batch, seq):
    m = p.shape[0]
    nt = seq // GLA_TB
    n_steps = batch * nt
    rowblk = lambda b, t: b * nt + t
    w_spec = pl.BlockSpec((D_MODEL // n_steps, D_MODEL), lambda b, t: (rowblk(b, t), 0))
    wbr_spec = lambda n: pl.BlockSpec((None, D_MODEL // n_steps, D_MODEL),
                                      lambda b, t: (n, rowblk(b, t), 0))
    wu_spec = pl.BlockSpec((D_MODEL // n_steps, D_FF), lambda b, t: (rowblk(b, t), 0))
    wd_spec = pl.BlockSpec((D_FF // n_steps, D_MODEL), lambda b, t: (rowblk(b, t), 0))
    w_shape = jax.ShapeDtypeStruct((D_MODEL, D_MODEL), BF16)
    return pl.pallas_call(
        _gla_kernel,
        grid=(batch, nt),
        in_specs=[
            pl.BlockSpec((GLA_TB, QK_W), lambda b, t: (rowblk(b, t), 0)),
            pl.BlockSpec((GLA_TB, QK_W), lambda b, t: (rowblk(b, t), COL_K // QK_W)),
            pl.BlockSpec((GLA_TB, V_W), lambda b, t: (rowblk(b, t), COL_V // V_W)),
            pl.BlockSpec((GLA_TB, V_W), lambda b, t: (rowblk(b, t), COL_R // V_W)),
            pl.BlockSpec((GLA_TB, LANES), lambda b, t: (rowblk(b, t), 0)),
            pl.BlockSpec((GLA_GATE_RANK, QK_W), lambda b, t: (0, 0)),
            pl.BlockSpec((1, QK_W), lambda b, t: (0, 0)),
            pl.BlockSpec((1, GLA_DV), lambda b, t: (0, 0)),
            wbr_spec(0), wbr_spec(1), w_spec, wu_spec, wd_spec,
        ],
        out_specs=[pl.BlockSpec((GLA_TB, V_W), lambda b, t: (rowblk(b, t), 0)),
                   w_spec, w_spec, w_spec, wu_spec, wd_spec],
        out_shape=[jax.ShapeDtypeStruct((m, V_W), BF16), w_shape, w_shape, w_shape,
                   jax.ShapeDtypeStruct((D_MODEL, D_FF), BF16),
                   jax.ShapeDtypeStruct((D_FF, D_MODEL), BF16)],
        scratch_shapes=[
            pltpu.VMEM((GLA_HEADS, GLA_DV, GLA_DK), F32),
            pltpu.VMEM((GLA_TB, QK_W), F32),
        ],
        compiler_params=pltpu.CompilerParams(
            dimension_semantics=("arbitrary", "arbitrary"), vmem_limit_bytes=VMEM_LIMIT),
        name="gla",
    )(q, p, p, p, glr, wup, b_alpha, gnw, w_branch, w_branch, w_out, w_ff_up, w_ff_down)


def _merge_kernel(og_ref, gu_ref, gv_ref, gate0_ref, gate1_ref, x_ref, ws_ref, bs_ref,
                  wb0_ref, wb1_ref, wo_ref, h_ref, gm_ref):
    pos_t = lax.broadcasted_iota(jnp.int32, (GMLP_BLOCK, GMLP_BLOCK), 0) // CHUNK
    pos_s = lax.broadcasted_iota(jnp.int32, (GMLP_BLOCK, GMLP_BLOCK), 1) // CHUNK
    causal = pos_t >= pos_s
    for g in range(GMLP_GROUPS):
        w = jnp.where(causal, ws_ref[g], 0.0).astype(BF16)
        cols = slice(g * GMLP_DG, (g + 1) * GMLP_DG)
        for blk in range(MERGE_TM // GMLP_BLOCK):
            rows = pl.ds(blk * GMLP_BLOCK, GMLP_BLOCK)
            mixed = jnp.dot(w, gv_ref[rows, cols], preferred_element_type=F32) + bs_ref[g]
            gm_ref[rows, cols] = (gu_ref[rows, cols].astype(F32) * mixed).astype(BF16)

    d0 = jnp.dot(og_ref[...], wb0_ref[...], preferred_element_type=F32)
    d1 = jnp.dot(gm_ref[...], wb1_ref[...], preferred_element_type=F32)
    mixed = (gate0_ref[...].astype(F32) * d0 + gate1_ref[...].astype(F32) * d1).astype(BF16)
    h_ref[...] = x_ref[...] + jnp.dot(mixed, wo_ref[...], preferred_element_type=F32)


def _merge(o_gla, p, x2, w_spatial, b_sp, wb0, wb1, wo):
    m = x2.shape[0]
    const2 = lambda i: (0, 0)
    const3 = lambda i: (0, 0, 0)
    single = pl.Buffered(1)
    return pl.pallas_call(
        _merge_kernel,
        grid=(m // MERGE_TM,),
        in_specs=[
            pl.BlockSpec((MERGE_TM, V_W), lambda i: (i, 0)),
            pl.BlockSpec((MERGE_TM, D_MODEL), lambda i: (i, COL_GU // D_MODEL)),
            pl.BlockSpec((MERGE_TM, D_MODEL), lambda i: (i, COL_GV // D_MODEL)),
            pl.BlockSpec((MERGE_TM, D_MODEL), lambda i: (i, COL_GATE // D_MODEL)),
            pl.BlockSpec((MERGE_TM, D_MODEL), lambda i: (i, COL_GATE // D_MODEL + 1)),
            pl.BlockSpec((MERGE_TM, D_MODEL), lambda i: (i, 0)),
            pl.BlockSpec((GMLP_GROUPS, GMLP_BLOCK, GMLP_BLOCK), const3, pipeline_mode=single),
            pl.BlockSpec((GMLP_GROUPS, GMLP_BLOCK, GMLP_DG), const3, pipeline_mode=single),
            pl.BlockSpec((V_W, D_MODEL), const2, pipeline_mode=single),
            pl.BlockSpec((V_W, D_MODEL), const2, pipeline_mode=single),
            pl.BlockSpec((D_MODEL, D_MODEL), const2, pipeline_mode=single),
        ],
        out_specs=pl.BlockSpec((MERGE_TM, D_MODEL), lambda i: (i, 0)),
        out_shape=jax.ShapeDtypeStruct((m, D_MODEL), F32),
        scratch_shapes=[pltpu.VMEM((MERGE_TM, D_MODEL), BF16)],
        compiler_params=pltpu.CompilerParams(
            dimension_semantics=("arbitrary",), vmem_limit_bytes=VMEM_LIMIT),
        name="merge",
    )(o_gla, p, p, p, p, x2, w_spatial, b_sp, wb0, wb1, wo)


def _ffn_kernel(h_ref, nw_ref, wu_ref, wd_ref, fw_ref, y_ref, hn_ref, acc_ref):
    f = pl.program_id(1)

    @pl.when(f == 0)
    def _():
        h = h_ref[...]
        ms = jnp.mean(h * h, axis=-1, keepdims=True)
        hn_ref[...] = (h * lax.rsqrt(ms + EPS) * nw_ref[...]).astype(BF16)
        acc_ref[...] = h

    up = jnp.dot(hn_ref[...], wu_ref[...], preferred_element_type=F32)
    up = jnp.square(jnp.maximum(up, 0.0)).astype(BF16)
    acc_ref[...] += jnp.dot(up, wd_ref[...], preferred_element_type=F32)

    @pl.when(f == pl.num_programs(1) - 1)
    def _():
        h2 = acc_ref[...]
        ms = jnp.mean(h2 * h2, axis=-1, keepdims=True)
        y_ref[...] = h2 * lax.rsqrt(ms + EPS) * fw_ref[...]


def _ffn(h, norm_w, wu, wd, final_w):
    m = h.shape[0]
    return pl.pallas_call(
        _ffn_kernel,
        grid=(m // FFN_TM, D_FF // FFN_TF),
        in_specs=[
            pl.BlockSpec((FFN_TM, D_MODEL), lambda i, f: (i, 0)),
            pl.BlockSpec((1, D_MODEL), lambda i, f: (0, 0)),
            pl.BlockSpec((D_MODEL, FFN_TF), lambda i, f: (0, f)),
            pl.BlockSpec((FFN_TF, D_MODEL), lambda i, f: (f, 0)),
            pl.BlockSpec((1, D_MODEL), lambda i, f: (0, 0)),
        ],
        out_specs=pl.BlockSpec((FFN_TM, D_MODEL), lambda i, f: (i, 0)),
        out_shape=jax.ShapeDtypeStruct((m, D_MODEL), F32),
        scratch_shapes=[
            pltpu.VMEM((FFN_TM, D_MODEL), BF16),
            pltpu.VMEM((FFN_TM, D_MODEL), F32),
        ],
        compiler_params=pltpu.CompilerParams(
            dimension_semantics=("arbitrary", "arbitrary"), vmem_limit_bytes=VMEM_LIMIT),
        name="ffn",
    )(h, norm_w, wu, wd, final_w)


def _layer(h2, batch, seq, norm_mix_w, w_in, w_alpha_up, b_alpha, gla_norm_w, gmlp_ln_w,
           gmlp_ln_b, w_spatial, b_spatial, b_gate, w_branch, w_out, norm_mlp_w, w_ff_up,
           w_ff_down, final_w):
    w_in_t = w_in.T
    xn, q, glr = _qnorm(h2, norm_mix_w.reshape(1, D_MODEL), w_in_t)
    p = _in_proj(xn, w_in_t, b_gate.reshape(1, 2 * D_MODEL), gmlp_ln_w.reshape(1, GMLP_DG),
                 gmlp_ln_b.reshape(1, GMLP_DG))
    o_gla, wb0_b, wb1_b, wo_b, wu_b, wd_b = _gla(
        q, p, glr, w_alpha_up.astype(BF16), b_alpha.reshape(1, QK_W),
        gla_norm_w.reshape(1, GLA_DV), w_branch, w_out, w_ff_up, w_ff_down, batch, seq)
    b_sp = jnp.broadcast_to(b_spatial[:, :, None], (GMLP_GROUPS, GMLP_BLOCK, GMLP_DG))
    h_mid = _merge(o_gla, p, h2, w_spatial, b_sp, wb0_b, wb1_b, wo_b)
    return _ffn(h_mid, norm_mlp_w.reshape(1, D_MODEL), wu_b, wd_b, final_w)


def kernel(x, norm_mix_w, w_in, w_alpha_up, b_alpha, gla_norm_w, gmlp_ln_w, gmlp_ln_b,
           w_spatial, b_spatial, b_gate, w_branch, w_out, norm_mlp_w, w_ff_up, w_ff_down,
           norm_final_w):
    batch, seq, d = x.shape
    depth = w_in.shape[0]
    assert d == D_MODEL and depth == 1 and seq % GLA_TB == 0
    assert w_in.shape[2] == D_IN
    h2 = x.reshape(batch * seq, d)
    y = _layer(h2, batch, seq, norm_mix_w[0], w_in[0], w_alpha_up[0], b_alpha[0],
               gla_norm_w[0], gmlp_ln_w[0], gmlp_ln_b[0], w_spatial[0], b_spatial[0],
               b_gate[0], w_branch[0], w_out[0], norm_mlp_w[0], w_ff_up[0], w_ff_down[0],
               norm_final_w.reshape(1, D_MODEL))
    return y.reshape(batch, seq, d)
```

```python
import math

import jax
import jax.numpy as jnp
from jax import lax
from jax.experimental import pallas as pl
from jax.experimental.pallas import tpu as pltpu

D_MODEL = 2048
CHUNK = 64
GLA_HEADS = 4
GLA_DK = D_MODEL // 8
GLA_DV = D_MODEL // 4
GLA_GATE_RANK = 16
GLA_TAU = 16.0
GMLP_BLOCK = 128
GMLP_GROUPS = 8
GMLP_DG = D_MODEL // GMLP_GROUPS
D_FF = 4 * D_MODEL
EPS = 1e-6
QK_W = GLA_HEADS * GLA_DK
V_W = GLA_HEADS * GLA_DV
D_IN = 2 * QK_W + 2 * V_W + GLA_GATE_RANK + 4 * D_MODEL

F32 = jnp.float32
BF16 = jnp.bfloat16

LANES = 128

WCOL_Q = 0
WCOL_K = QK_W
WCOL_V = 2 * QK_W
WCOL_R = WCOL_V + V_W
WCOL_GLR = WCOL_R + V_W
WCOL_GU = WCOL_GLR + GLA_GATE_RANK

COL_V = 0
COL_R = COL_V + V_W
COL_GU = COL_R + V_W
COL_GV = COL_GU + D_MODEL
COL_GATE = COL_GV + D_MODEL
COL_K = COL_GATE + 2 * D_MODEL
P_WIDTH = COL_K + QK_W

V7X_VMEM_BYTES = 64 * 1024 * 1024
VMEM_LIMIT = V7X_VMEM_BYTES - 8 * 1024 * 1024

QN_TM = 1024
IN_TM = 2048
IN_TN = 1024
IN_RC = 256
IN_UNROLL = 4
CAST_RC = 256
GLA_TB = 256
GLA_UNROLL = 4
MERGE_TM = 256
FFN_TM = 512
FFN_TF = 1024


def _sigmoid(x):
    return 0.5 * (1.0 + jnp.tanh(0.5 * x))


def _gelu_bf16(x):
    inner = math.sqrt(2.0 / math.pi) * (x + 0.044715 * (x * x * x))
    return (0.5 * x.astype(BF16)) * (1.0 + jnp.tanh(inner.astype(BF16)))


def _cast_transposed(wt_ref, wb_ref, n_rows):
    for rc in range(n_rows // CAST_RC):
        rows = pl.ds(rc * CAST_RC, CAST_RC)
        wb_ref[:, rc * CAST_RC:(rc + 1) * CAST_RC] = wt_ref[rows, :].T.astype(BF16)


def _qnorm_kernel(x_ref, nw_ref, wq_ref, wg_ref, xn_ref, q_ref, glr_ref, wqb_ref, wgb_ref):
    @pl.when(pl.program_id(0) == 0)
    def _():
        _cast_transposed(wq_ref, wqb_ref, QK_W)
        wgb_ref[...] = wg_ref[...].T.astype(BF16)

    for c in range(QN_TM // IN_RC):
        rows = pl.ds(c * IN_RC, IN_RC)
        x = x_ref[rows, :]
        ms = jnp.mean(x * x, axis=-1, keepdims=True)
        xn = (x * lax.rsqrt(ms + EPS) * nw_ref[...]).astype(BF16)
        xn_ref[rows, :] = xn
        q = jnp.dot(xn, wqb_ref[...], preferred_element_type=F32)
        q_ref[rows, :] = (q * (GLA_DK ** -0.5)).astype(BF16)
        glr_ref[rows, :] = jnp.dot(xn, wgb_ref[...], preferred_element_type=F32)


def _qnorm(x2, norm_w, w_in_t):
    m = x2.shape[0]
    single = pl.Buffered(1)
    return pl.pallas_call(
        _qnorm_kernel,
        grid=(m // QN_TM,),
        in_specs=[
            pl.BlockSpec((QN_TM, D_MODEL), lambda i: (i, 0)),
            pl.BlockSpec((1, D_MODEL), lambda i: (0, 0)),
            pl.BlockSpec((QK_W, D_MODEL), lambda i: (WCOL_Q // QK_W, 0), pipeline_mode=single),
            pl.BlockSpec((LANES, D_MODEL), lambda i: (WCOL_GLR // LANES, 0), pipeline_mode=single),
        ],
        out_specs=[
            pl.BlockSpec((QN_TM, D_MODEL), lambda i: (i, 0)),
            pl.BlockSpec((QN_TM, QK_W), lambda i: (i, 0)),
            pl.BlockSpec((QN_TM, LANES), lambda i: (i, 0)),
        ],
        out_shape=[
            jax.ShapeDtypeStruct((m, D_MODEL), BF16),
            jax.ShapeDtypeStruct((m, QK_W), BF16),
            jax.ShapeDtypeStruct((m, LANES), F32),
        ],
        scratch_shapes=[pltpu.VMEM((D_MODEL, QK_W), BF16), pltpu.VMEM((D_MODEL, LANES), BF16)],
        compiler_params=pltpu.CompilerParams(
            dimension_semantics=("arbitrary",), vmem_limit_bytes=VMEM_LIMIT),
        name="qnorm",
    )(x2, norm_w, w_in_t, w_in_t)


_T_R, _T_GU, _T_GV, _T_GATE, _T_K = (c // IN_TN for c in (COL_R, COL_GU, COL_GV, COL_GATE, COL_K))
_N_T = P_WIDTH // IN_TN


def _in_proj_kernel(xn_ref, w_ref, bg_ref, lnw_ref, lnb_ref, p_ref, wb_ref):
    t = pl.program_id(0)
    i = pl.program_id(1)

    @pl.when(i == 0)
    def _():
        _cast_transposed(w_ref, wb_ref, IN_TN)

    def run(epilogue):
        def chunk_body(c, carry):
            rows = pl.ds(pl.multiple_of(c * IN_RC, IN_RC), IN_RC)
            acc = jnp.dot(xn_ref[rows, :], wb_ref[...], preferred_element_type=F32)
            p_ref[rows, :] = epilogue(acc).astype(BF16)
            return carry

        lax.fori_loop(0, IN_TM // IN_RC, chunk_body, 0, unroll=IN_UNROLL)

    def epi_ln(acc):
        outs = []
        for g in range(IN_TN // GMLP_DG):
            cols = slice(g * GMLP_DG, (g + 1) * GMLP_DG)
            v = _gelu_bf16(acc[:, cols]).astype(F32)
            mu = jnp.mean(v, axis=-1, keepdims=True)
            vc = v - mu
            var = jnp.mean(vc * vc, axis=-1, keepdims=True)
            outs.append(vc * lax.rsqrt(var + EPS) * lnw_ref[...] + lnb_ref[...])
        return jnp.concatenate(outs, axis=-1)

    @pl.when(jnp.logical_or(t < _T_R, t >= _T_K))
    def _():
        run(lambda a: a)

    @pl.when(jnp.logical_and(t >= _T_R, t < _T_GU))
    def _():
        run(lambda a: a.astype(BF16) * _sigmoid(a.astype(BF16)))

    @pl.when(jnp.logical_and(t >= _T_GU, t < _T_GV))
    def _():
        run(_gelu_bf16)

    @pl.when(jnp.logical_and(t >= _T_GV, t < _T_GATE))
    def _():
        run(epi_ln)

    @pl.when(jnp.logical_and(t >= _T_GATE, t < _T_K))
    def _():
        run(lambda a: _sigmoid((a + bg_ref[...]).astype(BF16)))


def _in_proj(xn, w_in_t, b_gate, ln_w, ln_b):
    m = xn.shape[0]
    n_i = m // IN_TM
    n_gate_blocks = (2 * D_MODEL) // IN_TN

    def w_rows(t, i):
        col = jnp.where(t < _T_GU, WCOL_V + t * IN_TN,
                        jnp.where(t < _T_K, WCOL_GU + (t - _T_GU) * IN_TN, WCOL_K))
        return (pl.multiple_of(col, GLA_GATE_RANK), 0)

    return pl.pallas_call(
        _in_proj_kernel,
        grid=(_N_T, n_i),
        in_specs=[
            pl.BlockSpec((IN_TM, D_MODEL), lambda t, i: (i, 0)),
            pl.BlockSpec((pl.Element(IN_TN), pl.Element(D_MODEL)), w_rows),
            pl.BlockSpec((1, IN_TN),
                         lambda t, i: (0, jnp.clip(t - _T_GATE, 0, n_gate_blocks - 1))),
            pl.BlockSpec((1, GMLP_DG), lambda t, i: (0, 0)),
            pl.BlockSpec((1, GMLP_DG), lambda t, i: (0, 0)),
        ],
        out_specs=pl.BlockSpec((IN_TM, IN_TN), lambda t, i: (i, t)),
        out_shape=jax.ShapeDtypeStruct((m, P_WIDTH), BF16),
        scratch_shapes=[pltpu.VMEM((D_MODEL, IN_TN), BF16)],
        compiler_params=pltpu.CompilerParams(
            dimension_semantics=("arbitrary", "arbitrary"), vmem_limit_bytes=VMEM_LIMIT),
        name="in_proj",
    )(xn, w_in_t, b_gate, ln_w, ln_b)


def _gla_kernel(q_ref, k_ref, v_ref, rs_ref, glr_ref, wup_ref, ba_ref, gnw_ref,
                wb0_ref, wb1_ref, wo_ref, wu_ref, wd_ref,
                o_ref, wb0b_ref, wb1b_ref, wob_ref, wub_ref, wdb_ref, state_ref, z_ref):
    t = pl.program_id(1)

    @pl.when(t == 0)
    def _():
        state_ref[...] = jnp.zeros_like(state_ref)

    glr = glr_ref[:, :GLA_GATE_RANK].astype(BF16)
    z_ref[...] = jnp.dot(glr, wup_ref[...], preferred_element_type=F32) + ba_ref[...]

    row = lax.broadcasted_iota(jnp.int32, (CHUNK, CHUNK), 0)
    col = lax.broadcasted_iota(jnp.int32, (CHUNK, CHUNK), 1)
    tri = (row >= col).astype(BF16)

    n_chunks = GLA_TB // CHUNK
    w_rows = wb0_ref.shape[0] // n_chunks
    wu_rows = wu_ref.shape[0] // n_chunks
    wd_rows = wd_ref.shape[0] // n_chunks

    def chunk_body(c, carry):
        for src, dst, n in ((wb0_ref, wb0b_ref, w_rows), (wb1_ref, wb1b_ref, w_rows),
                            (wo_ref, wob_ref, w_rows), (wu_ref, wub_ref, wu_rows),
                            (wd_ref, wdb_ref, wd_rows)):
            wr = pl.ds(pl.multiple_of(c * n, n), n)
            dst[wr, :] = src[wr, :].astype(BF16)

        rows = pl.ds(pl.multiple_of(c * CHUNK, CHUNK), CHUNK)
        z = z_ref[rows, :]
        la = (jnp.minimum(z, 0.0) - jnp.log1p(jnp.exp(-jnp.abs(z)))) * (1.0 / GLA_TAU)
        la_hi = la.astype(BF16)
        la_lo = (la - la_hi.astype(F32)).astype(BF16)
        lcum = (jnp.dot(tri, la_hi, preferred_element_type=F32)
                + jnp.dot(tri, la_lo, preferred_element_type=F32))
        l_end = lcum[CHUNK - 1:CHUNK, :]
        k_dec = (k_ref[rows, :].astype(F32) * jnp.exp(l_end - lcum)).astype(BF16)
        decay = jnp.exp(l_end)
        q = q_ref[rows, :]
        v = v_ref[rows, :]
        rs = rs_ref[rows, :]
        for h in range(GLA_HEADS):
            kc = slice(h * GLA_DK, (h + 1) * GLA_DK)
            vc = slice(h * GLA_DV, (h + 1) * GLA_DV)
            kv_t = lax.dot_general(v[:, vc], k_dec[:, kc], (((0,), (0,)), ((), ())),
                                   preferred_element_type=F32)
            s = state_ref[h] * decay[:, kc] + kv_t
            state_ref[h] = s
            o = lax.dot_general(q[:, kc], s.astype(BF16), (((1,), (1,)), ((), ())),
                                preferred_element_type=F32)
            ms = jnp.mean(o * o, axis=-1, keepdims=True)
            o = o * lax.rsqrt(ms + EPS) * gnw_ref[...] * rs[:, vc].astype(F32)
            o_ref[rows, vc] = o.astype(BF16)
        return carry

    lax.fori_loop(0, n_chunks, chunk_body, 0, unroll=GLA_UNROLL)


def _gla(q, p, glr, wup, b_alpha, gnw, w_branch, w_out, w_ff_up, w_ff_down, batch, seq):
    m = p.shape[0]
    nt = seq // GLA_TB
    n_steps = batch * nt
    rowblk = lambda b, t: b * nt + t
    w_spec = pl.BlockSpec((D_MODEL // n_steps, D_MODEL), lambda b, t: (rowblk(b, t), 0))
    wbr_spec = lambda n: pl.BlockSpec((None, D_MODEL // n_steps, D_MODEL),
                                      lambda b, t: (n, rowblk(b, t), 0))
    wu_spec = pl.BlockSpec((D_MODEL // n_steps, D_FF), lambda b, t: (rowblk(b, t), 0))
    wd_spec = pl.BlockSpec((D_FF // n_steps, D_MODEL), lambda b, t: (rowblk(b, t), 0))
    w_shape = jax.ShapeDtypeStruct((D_MODEL, D_MODEL), BF16)
    return pl.pallas_call(
        _gla_kernel,
        grid=(batch, nt),
        in_specs=[
            pl.BlockSpec((GLA_TB, QK_W), lambda b, t: (rowblk(b, t), 0)),
            pl.BlockSpec((GLA_TB, QK_W), lambda b, t: (rowblk(b, t), COL_K // QK_W)),
            pl.BlockSpec((GLA_TB, V_W), lambda b, t: (rowblk(b, t), COL_V // V_W)),
            pl.BlockSpec((GLA_TB, V_W), lambda b, t: (rowblk(b, t), COL_R // V_W)),
            pl.BlockSpec((GLA_TB, LANES), lambda b, t: (rowblk(b, t), 0)),
            pl.BlockSpec((GLA_GATE_RANK, QK_W), lambda b, t: (0, 0)),
            pl.BlockSpec((1, QK_W), lambda b, t: (0, 0)),
            pl.BlockSpec((1, GLA_DV), lambda b, t: (0, 0)),
            wbr_spec(0), wbr_spec(1), w_spec, wu_spec, wd_spec,
        ],
        out_specs=[pl.BlockSpec((GLA_TB, V_W), lambda b, t: (rowblk(b, t), 0)),
                   w_spec, w_spec, w_spec, wu_spec, wd_spec],
        out_shape=[jax.ShapeDtypeStruct((m, V_W), BF16), w_shape, w_shape, w_shape,
                   jax.ShapeDtypeStruct((D_MODEL, D_FF), BF16),
                   jax.ShapeDtypeStruct((D_FF, D_MODEL), BF16)],
        scratch_shapes=[
            pltpu.VMEM((GLA_HEADS, GLA_DV, GLA_DK), F32),
            pltpu.VMEM((GLA_TB, QK_W), F32),
        ],
        compiler_params=pltpu.CompilerParams(
            dimension_semantics=("arbitrary", "arbitrary"), vmem_limit_bytes=VMEM_LIMIT),
        name="gla",
    )(q, p, p, p, glr, wup, b_alpha, gnw, w_branch, w_branch, w_out, w_ff_up, w_ff_down)


def _merge_kernel(og_ref, gu_ref, gv_ref, gate0_ref, gate1_ref, x_ref, ws_ref, bs_ref,
                  wb0_ref, wb1_ref, wo_ref, h_ref, gm_ref):
    pos_t = lax.broadcasted_iota(jnp.int32, (GMLP_BLOCK, GMLP_BLOCK), 0) // CHUNK
    pos_s = lax.broadcasted_iota(jnp.int32, (GMLP_BLOCK, GMLP_BLOCK), 1) // CHUNK
    causal = pos_t >= pos_s
    for g in range(GMLP_GROUPS):
        w = jnp.where(causal, ws_ref[g], 0.0).astype(BF16)
        cols = slice(g * GMLP_DG, (g + 1) * GMLP_DG)
        for blk in range(MERGE_TM // GMLP_BLOCK):
            rows = pl.ds(blk * GMLP_BLOCK, GMLP_BLOCK)
            mixed = jnp.dot(w, gv_ref[rows, cols], preferred_element_type=F32) + bs_ref[g]
            gm_ref[rows, cols] = (gu_ref[rows, cols].astype(F32) * mixed).astype(BF16)

    d0 = jnp.dot(og_ref[...], wb0_ref[...], preferred_element_type=F32)
    d1 = jnp.dot(gm_ref[...], wb1_ref[...], preferred_element_type=F32)
    mixed = (gate0_ref[...].astype(F32) * d0 + gate1_ref[...].astype(F32) * d1).astype(BF16)
    h_ref[...] = x_ref[...] + jnp.dot(mixed, wo_ref[...], preferred_element_type=F32)


def _merge(o_gla, p, x2, w_spatial, b_sp, wb0, wb1, wo):
    m = x2.shape[0]
    const2 = lambda i: (0, 0)
    const3 = lambda i: (0, 0, 0)
    single = pl.Buffered(1)
    return pl.pallas_call(
        _merge_kernel,
        grid=(m // MERGE_TM,),
        in_specs=[
            pl.BlockSpec((MERGE_TM, V_W), lambda i: (i, 0)),
            pl.BlockSpec((MERGE_TM, D_MODEL), lambda i: (i, COL_GU // D_MODEL)),
            pl.BlockSpec((MERGE_TM, D_MODEL), lambda i: (i, COL_GV // D_MODEL)),
            pl.BlockSpec((MERGE_TM, D_MODEL), lambda i: (i, COL_GATE // D_MODEL)),
            pl.BlockSpec((MERGE_TM, D_MODEL), lambda i: (i, COL_GATE // D_MODEL + 1)),
            pl.BlockSpec((MERGE_TM, D_MODEL), lambda i: (i, 0)),
            pl.BlockSpec((GMLP_GROUPS, GMLP_BLOCK, GMLP_BLOCK), const3, pipeline_mode=single),
            pl.BlockSpec((GMLP_GROUPS, GMLP_BLOCK, GMLP_DG), const3, pipeline_mode=single),
            pl.BlockSpec((V_W, D_MODEL), const2, pipeline_mode=single),
            pl.BlockSpec((V_W, D_MODEL), const2, pipeline_mode=single),
            pl.BlockSpec((D_MODEL, D_MODEL), const2, pipeline_mode=single),
        ],
        out_specs=pl.BlockSpec((MERGE_TM, D_MODEL), lambda i: (i, 0)),
        out_shape=jax.ShapeDtypeStruct((m, D_MODEL), F32),
        scratch_shapes=[pltpu.VMEM((MERGE_TM, D_MODEL), BF16)],
        compiler_params=pltpu.CompilerParams(
            dimension_semantics=("arbitrary",), vmem_limit_bytes=VMEM_LIMIT),
        name="merge",
    )(o_gla, p, p, p, p, x2, w_spatial, b_sp, wb0, wb1, wo)


def _ffn_kernel(h_ref, nw_ref, wu_ref, wd_ref, fw_ref, y_ref, hn_ref, acc_ref):
    f = pl.program_id(1)

    @pl.when(f == 0)
    def _():
        h = h_ref[...]
        ms = jnp.mean(h * h, axis=-1, keepdims=True)
        hn_ref[...] = (h * lax.rsqrt(ms + EPS) * nw_ref[...]).astype(BF16)
        acc_ref[...] = h

    up = jnp.dot(hn_ref[...], wu_ref[...], preferred_element_type=F32)
    up = jnp.square(jnp.maximum(up, 0.0)).astype(BF16)
    acc_ref[...] += jnp.dot(up, wd_ref[...], preferred_element_type=F32)

    @pl.when(f == pl.num_programs(1) - 1)
    def _():
        h2 = acc_ref[...]
        ms = jnp.mean(h2 * h2, axis=-1, keepdims=True)
        y_ref[...] = h2 * lax.rsqrt(ms + EPS) * fw_ref[...]


def _ffn(h, norm_w, wu, wd, final_w):
    m = h.shape[0]
    return pl.pallas_call(
        _ffn_kernel,
        grid=(m // FFN_TM, D_FF // FFN_TF),
        in_specs=[
            pl.BlockSpec((FFN_TM, D_MODEL), lambda i, f: (i, 0)),
            pl.BlockSpec((1, D_MODEL), lambda i, f: (0, 0)),
            pl.BlockSpec((D_MODEL, FFN_TF), lambda i, f: (0, f)),
            pl.BlockSpec((FFN_TF, D_MODEL), lambda i, f: (f, 0)),
            pl.BlockSpec((1, D_MODEL), lambda i, f: (0, 0)),
        ],
        out_specs=pl.BlockSpec((FFN_TM, D_MODEL), lambda i, f: (i, 0)),
        out_shape=jax.ShapeDtypeStruct((m, D_MODEL), F32),
        scratch_shapes=[
            pltpu.VMEM((FFN_TM, D_MODEL), BF16),
            pltpu.VMEM((FFN_TM, D_MODEL), F32),
        ],
        compiler_params=pltpu.CompilerParams(
            dimension_semantics=("arbitrary", "arbitrary"), vmem_limit_bytes=VMEM_LIMIT),
        name="ffn",
    )(h, norm_w, wu, wd, final_w)


def _layer(h2, batch, seq, norm_mix_w, w_in, w_alpha_up, b_alpha, gla_norm_w, gmlp_ln_w,
           gmlp_ln_b, w_spatial, b_spatial, b_gate, w_branch, w_out, norm_mlp_w, w_ff_up,
           w_ff_down, final_w):
    w_in_t = w_in.T
    xn, q, glr = _qnorm(h2, norm_mix_w.reshape(1, D_MODEL), w_in_t)
    p = _in_proj(xn, w_in_t, b_gate.reshape(1, 2 * D_MODEL), gmlp_ln_w.reshape(1, GMLP_DG),
                 gmlp_ln_b.reshape(1, GMLP_DG))
    o_gla, wb0_b, wb1_b, wo_b, wu_b, wd_b = _gla(
        q, p, glr, w_alpha_up.astype(BF16), b_alpha.reshape(1, QK_W),
        gla_norm_w.reshape(1, GLA_DV), w_branch, w_out, w_ff_up, w_ff_down, batch, seq)
    b_sp = jnp.broadcast_to(b_spatial[:, :, None], (GMLP_GROUPS, GMLP_BLOCK, GMLP_DG))
    h_mid = _merge(o_gla, p, h2, w_spatial, b_sp, wb0_b, wb1_b, wo_b)
    return _ffn(h_mid, norm_mlp_w.reshape(1, D_MODEL), wu_b, wd_b, final_w)


def kernel(x, norm_mix_w, w_in, w_alpha_up, b_alpha, gla_norm_w, gmlp_ln_w, gmlp_ln_b,
           w_spatial, b_spatial, b_gate, w_branch, w_out, norm_mlp_w, w_ff_up, w_ff_down,
           norm_final_w):
    batch, seq, d = x.shape
    depth = w_in.shape[0]
    assert d == D_MODEL and depth == 1 and seq % GLA_TB == 0
    assert w_in.shape[2] == D_IN
    h2 = x.reshape(batch * seq, d)
    y = _layer(h2, batch, seq, norm_mix_w[0], w_in[0], w_alpha_up[0], b_alpha[0],
               gla_norm_w[0], gmlp_ln_w[0], gmlp_ln_b[0], w_spatial[0], b_spatial[0],
               b_gate[0], w_branch[0], w_out[0], norm_mlp_w[0], w_ff_up[0], w_ff_down[0],
               norm_final_w.reshape(1, D_MODEL))
    return y.reshape(batch, seq, d)
```

```python
import math

import jax
import jax.numpy as jnp
from jax import lax
from jax.experimental import pallas as pl
from jax.experimental.pallas import tpu as pltpu

D_MODEL = 2048
CHUNK = 64
GLA_HEADS = 4
GLA_DK = D_MODEL // 8
GLA_DV = D_MODEL // 4
GLA_GATE_RANK = 16
GLA_TAU = 16.0
GMLP_BLOCK = 128
GMLP_GROUPS = 8
GMLP_DG = D_MODEL // GMLP_GROUPS
D_FF = 4 * D_MODEL
EPS = 1e-6
QK_W = GLA_HEADS * GLA_DK
V_W = GLA_HEADS * GLA_DV
D_IN = 2 * QK_W + 2 * V_W + GLA_GATE_RANK + 4 * D_MODEL

F32 = jnp.float32
BF16 = jnp.bfloat16

LANES = 128

WCOL_Q = 0
WCOL_K = QK_W
WCOL_V = 2 * QK_W
WCOL_R = WCOL_V + V_W
WCOL_GLR = WCOL_R + V_W
WCOL_GU = WCOL_GLR + GLA_GATE_RANK

COL_V = 0
COL_R = COL_V + V_W
COL_GU = COL_R + V_W
COL_GV = COL_GU + D_MODEL
COL_GATE = COL_GV + D_MODEL
COL_K = COL_GATE + 2 * D_MODEL
P_WIDTH = COL_K + QK_W

V7X_VMEM_BYTES = 64 * 1024 * 1024
VMEM_LIMIT = V7X_VMEM_BYTES - 8 * 1024 * 1024

QN_TM = 1024
IN_TM = 2048
IN_TN = 1024
IN_RC = 256
IN_UNROLL = 4
CAST_RC = 256
GLA_TB = 256
GLA_UNROLL = 4
MERGE_TM = 256
FFN_TM = 512
FFN_TF = 1024


def _sigmoid(x):
    return 0.5 * (1.0 + jnp.tanh(0.5 * x))


def _gelu_bf16(x):
    inner = math.sqrt(2.0 / math.pi) * (x + 0.044715 * (x * x * x))
    return (0.5 * x.astype(BF16)) * (1.0 + jnp.tanh(inner.astype(BF16)))


def _cast_transposed(wt_ref, wb_ref, n_rows):
    for rc in range(n_rows // CAST_RC):
        rows = pl.ds(rc * CAST_RC, CAST_RC)
        wb_ref[:, rc * CAST_RC:(rc + 1) * CAST_RC] = wt_ref[rows, :].T.astype(BF16)


def _qnorm_kernel(x_ref, nw_ref, wq_ref, wg_ref, xn_ref, q_ref, glr_ref, wqg_ref):
    @pl.when(pl.program_id(0) == 0)
    def _():
        _cast_transposed(wq_ref, wqg_ref, QK_W)
        wqg_ref[:, QK_W:] = wg_ref[...].T.astype(BF16)

    for c in range(QN_TM // IN_RC):
        rows = pl.ds(c * IN_RC, IN_RC)
        x = x_ref[rows, :]
        ms = jnp.mean(x * x, axis=-1, keepdims=True)
        xn = (x * lax.rsqrt(ms + EPS) * nw_ref[...]).astype(BF16)
        xn_ref[rows, :] = xn
        qg = jnp.dot(xn, wqg_ref[...], preferred_element_type=F32)
        q_ref[rows, :] = (qg[:, :QK_W] * (GLA_DK ** -0.5)).astype(BF16)
        glr_ref[rows, :] = qg[:, QK_W:]


def _qnorm(x2, norm_w, w_in_t):
    m = x2.shape[0]
    single = pl.Buffered(1)
    return pl.pallas_call(
        _qnorm_kernel,
        grid=(m // QN_TM,),
        in_specs=[
            pl.BlockSpec((QN_TM, D_MODEL), lambda i: (i, 0)),
            pl.BlockSpec((1, D_MODEL), lambda i: (0, 0)),
            pl.BlockSpec((QK_W, D_MODEL), lambda i: (WCOL_Q // QK_W, 0), pipeline_mode=single),
            pl.BlockSpec((LANES, D_MODEL), lambda i: (WCOL_GLR // LANES, 0), pipeline_mode=single),
        ],
        out_specs=[
            pl.BlockSpec((QN_TM, D_MODEL), lambda i: (i, 0)),
            pl.BlockSpec((QN_TM, QK_W), lambda i: (i, 0)),
            pl.BlockSpec((QN_TM, LANES), lambda i: (i, 0)),
        ],
        out_shape=[
            jax.ShapeDtypeStruct((m, D_MODEL), BF16),
            jax.ShapeDtypeStruct((m, QK_W), BF16),
            jax.ShapeDtypeStruct((m, LANES), F32),
        ],
        scratch_shapes=[pltpu.VMEM((D_MODEL, QK_W + LANES), BF16)],
        compiler_params=pltpu.CompilerParams(
            dimension_semantics=("arbitrary",), vmem_limit_bytes=VMEM_LIMIT),
        name="qnorm",
    )(x2, norm_w, w_in_t, w_in_t)


_T_R, _T_GU, _T_GV, _T_GATE, _T_K = (c // IN_TN for c in (COL_R, COL_GU, COL_GV, COL_GATE, COL_K))
_N_T = P_WIDTH // IN_TN


def _in_proj_kernel(xn_ref, w_ref, bg_ref, lnw_ref, lnb_ref, p_ref, wb_ref):
    t = pl.program_id(0)
    i = pl.program_id(1)

    @pl.when(i == 0)
    def _():
        _cast_transposed(w_ref, wb_ref, IN_TN)

    def run(epilogue):
        def chunk_body(c, carry):
            rows = pl.ds(pl.multiple_of(c * IN_RC, IN_RC), IN_RC)
            acc = jnp.dot(xn_ref[rows, :], wb_ref[...], preferred_element_type=F32)
            p_ref[rows, :] = epilogue(acc).astype(BF16)
            return carry

        lax.fori_loop(0, IN_TM // IN_RC, chunk_body, 0, unroll=IN_UNROLL)

    def epi_ln(acc):
        outs = []
        for g in range(IN_TN // GMLP_DG):
            cols = slice(g * GMLP_DG, (g + 1) * GMLP_DG)
            v = _gelu_bf16(acc[:, cols]).astype(F32)
            mu = jnp.mean(v, axis=-1, keepdims=True)
            vc = v - mu
            var = jnp.mean(vc * vc, axis=-1, keepdims=True)
            outs.append(vc * lax.rsqrt(var + EPS) * lnw_ref[...] + lnb_ref[...])
        return jnp.concatenate(outs, axis=-1)

    @pl.when(jnp.logical_or(t < _T_R, t >= _T_K))
    def _():
        run(lambda a: a)

    @pl.when(jnp.logical_and(t >= _T_R, t < _T_GU))
    def _():
        run(lambda a: a.astype(BF16) * _sigmoid(a.astype(BF16)))

    @pl.when(jnp.logical_and(t >= _T_GU, t < _T_GV))
    def _():
        run(_gelu_bf16)

    @pl.when(jnp.logical_and(t >= _T_GV, t < _T_GATE))
    def _():
        run(epi_ln)

    @pl.when(jnp.logical_and(t >= _T_GATE, t < _T_K))
    def _():
        run(lambda a: _sigmoid((a + bg_ref[...]).astype(BF16)))


def _in_proj(xn, w_in_t, b_gate, ln_w, ln_b):
    m = xn.shape[0]
    n_i = m // IN_TM
    n_gate_blocks = (2 * D_MODEL) // IN_TN

    def w_rows(t, i):
        col = jnp.where(t < _T_GU, WCOL_V + t * IN_TN,
                        jnp.where(t < _T_K, WCOL_GU + (t - _T_GU) * IN_TN, WCOL_K))
        return (pl.multiple_of(col, GLA_GATE_RANK), 0)

    return pl.pallas_call(
        _in_proj_kernel,
        grid=(_N_T, n_i),
        in_specs=[
            pl.BlockSpec((IN_TM, D_MODEL), lambda t, i: (i, 0)),
            pl.BlockSpec((pl.Element(IN_TN), pl.Element(D_MODEL)), w_rows),
            pl.BlockSpec((1, IN_TN),
                         lambda t, i: (0, jnp.clip(t - _T_GATE, 0, n_gate_blocks - 1))),
            pl.BlockSpec((1, GMLP_DG), lambda t, i: (0, 0)),
            pl.BlockSpec((1, GMLP_DG), lambda t, i: (0, 0)),
        ],
        out_specs=pl.BlockSpec((IN_TM, IN_TN), lambda t, i: (i, t)),
        out_shape=jax.ShapeDtypeStruct((m, P_WIDTH), BF16),
        scratch_shapes=[pltpu.VMEM((D_MODEL, IN_TN), BF16)],
        compiler_params=pltpu.CompilerParams(
            dimension_semantics=("arbitrary", "arbitrary"), vmem_limit_bytes=VMEM_LIMIT),
        name="in_proj",
    )(xn, w_in_t, b_gate, ln_w, ln_b)


def _gla_kernel(q_ref, k_ref, vr_ref, glr_ref, wup_ref, ba_ref, gnw_ref,
                wb0_ref, wb1_ref, wo_ref, wu_ref, wd_ref,
                o_ref, wb0b_ref, wb1b_ref, wob_ref, wub_ref, wdb_ref, state_ref, z_ref):
    t = pl.program_id(1)

    @pl.when(t == 0)
    def _():
        state_ref[...] = jnp.zeros_like(state_ref)

    glr = glr_ref[:, :GLA_GATE_RANK].astype(BF16)
    z_ref[...] = jnp.dot(glr, wup_ref[...], preferred_element_type=F32) + ba_ref[...]

    row = lax.broadcasted_iota(jnp.int32, (CHUNK, CHUNK), 0)
    col = lax.broadcasted_iota(jnp.int32, (CHUNK, CHUNK), 1)
    tri = (row >= col).astype(BF16)
    tri2 = jnp.concatenate([tri, tri], axis=1)

    n_chunks = GLA_TB // CHUNK
    w_rows = wb0_ref.shape[0] // n_chunks
    wu_rows = wu_ref.shape[0] // n_chunks
    wd_rows = wd_ref.shape[0] // n_chunks

    def chunk_body(c, carry):
        for src, dst, n in ((wb0_ref, wb0b_ref, w_rows), (wb1_ref, wb1b_ref, w_rows),
                            (wo_ref, wob_ref, w_rows), (wu_ref, wub_ref, wu_rows),
                            (wd_ref, wdb_ref, wd_rows)):
            wr = pl.ds(pl.multiple_of(c * n, n), n)
            dst[wr, :] = src[wr, :].astype(BF16)

        rows = pl.ds(pl.multiple_of(c * CHUNK, CHUNK), CHUNK)
        z = z_ref[rows, :]
        la = (jnp.minimum(z, 0.0) - jnp.log1p(jnp.exp(-jnp.abs(z)))) * (1.0 / GLA_TAU)
        la_hi = la.astype(BF16)
        la_lo = (la - la_hi.astype(F32)).astype(BF16)
        lcum = jnp.dot(tri2, jnp.concatenate([la_hi, la_lo], axis=0),
                       preferred_element_type=F32)
        l_end = lcum[CHUNK - 1:CHUNK, :]
        k_dec = (k_ref[rows, :].astype(F32) * jnp.exp(l_end - lcum)).astype(BF16)
        decay = jnp.exp(l_end)
        q = q_ref[rows, :]
        v = vr_ref[rows, :V_W]
        rs = vr_ref[rows, V_W:]
        for h in range(GLA_HEADS):
            kc = slice(h * GLA_DK, (h + 1) * GLA_DK)
            vc = slice(h * GLA_DV, (h + 1) * GLA_DV)
            kv_t = lax.dot_general(v[:, vc], k_dec[:, kc], (((0,), (0,)), ((), ())),
                                   preferred_element_type=F32)
            s = state_ref[h] * decay[:, kc] + kv_t
            state_ref[h] = s
            o = lax.dot_general(q[:, kc], s.astype(BF16), (((1,), (1,)), ((), ())),
                                preferred_element_type=F32)
            ms = jnp.mean(o * o, axis=-1, keepdims=True)
            o = o * lax.rsqrt(ms + EPS) * gnw_ref[...] * rs[:, vc].astype(F32)
            o_ref[rows, vc] = o.astype(BF16)
        return carry

    lax.fori_loop(0, n_chunks, chunk_body, 0, unroll=GLA_UNROLL)


def _gla(q, p, glr, wup, b_alpha, gnw, w_branch, w_out, w_ff_up, w_ff_down, batch, seq):
    m = p.shape[0]
    nt = seq // GLA_TB
    n_steps = batch * nt
    rowblk = lambda b, t: b * nt + t
    w_spec = pl.BlockSpec((D_MODEL // n_steps, D_MODEL), lambda b, t: (rowblk(b, t), 0))
    wbr_spec = lambda n: pl.BlockSpec((None, D_MODEL // n_steps, D_MODEL),
                                      lambda b, t: (n, rowblk(b, t), 0))
    wu_spec = pl.BlockSpec((D_MODEL // n_steps, D_FF), lambda b, t: (rowblk(b, t), 0))
    wd_spec = pl.BlockSpec((D_FF // n_steps, D_MODEL), lambda b, t: (rowblk(b, t), 0))
    w_shape = jax.ShapeDtypeStruct((D_MODEL, D_MODEL), BF16)
    return pl.pallas_call(
        _gla_kernel,
        grid=(batch, nt),
        in_specs=[
            pl.BlockSpec((GLA_TB, QK_W), lambda b, t: (rowblk(b, t), 0)),
            pl.BlockSpec((GLA_TB, QK_W), lambda b, t: (rowblk(b, t), COL_K // QK_W)),
            pl.BlockSpec((GLA_TB, 2 * V_W), lambda b, t: (rowblk(b, t), COL_V // (2 * V_W))),
            pl.BlockSpec((GLA_TB, LANES), lambda b, t: (rowblk(b, t), 0)),
            pl.BlockSpec((GLA_GATE_RANK, QK_W), lambda b, t: (0, 0)),
            pl.BlockSpec((1, QK_W), lambda b, t: (0, 0)),
            pl.BlockSpec((1, GLA_DV), lambda b, t: (0, 0)),
            wbr_spec(0), wbr_spec(1), w_spec, wu_spec, wd_spec,
        ],
        out_specs=[pl.BlockSpec((GLA_TB, V_W), lambda b, t: (rowblk(b, t), 0)),
                   w_spec, w_spec, w_spec, wu_spec, wd_spec],
        out_shape=[jax.ShapeDtypeStruct((m, V_W), BF16), w_shape, w_shape, w_shape,
                   jax.ShapeDtypeStruct((D_MODEL, D_FF), BF16),
                   jax.ShapeDtypeStruct((D_FF, D_MODEL), BF16)],
        scratch_shapes=[
            pltpu.VMEM((GLA_HEADS, GLA_DV, GLA_DK), F32),
            pltpu.VMEM((GLA_TB, QK_W), F32),
        ],
        compiler_params=pltpu.CompilerParams(
            dimension_semantics=("arbitrary", "arbitrary"), vmem_limit_bytes=VMEM_LIMIT),
        name="gla",
    )(q, p, p, glr, wup, b_alpha, gnw, w_branch, w_branch, w_out, w_ff_up, w_ff_down)


def _merge_kernel(og_ref, guv_ref, gate_ref, x_ref, ws_ref, bs_ref,
                  wb0_ref, wb1_ref, wo_ref, h_ref, gm_ref):
    pos_t = lax.broadcasted_iota(jnp.int32, (GMLP_BLOCK, GMLP_BLOCK), 0) // CHUNK
    pos_s = lax.broadcasted_iota(jnp.int32, (GMLP_BLOCK, GMLP_BLOCK), 1) // CHUNK
    causal = pos_t >= pos_s
    for g in range(GMLP_GROUPS):
        w = jnp.where(causal, ws_ref[g], 0.0).astype(BF16)
        cols = slice(g * GMLP_DG, (g + 1) * GMLP_DG)
        vcols = slice(D_MODEL + g * GMLP_DG, D_MODEL + (g + 1) * GMLP_DG)
        for blk in range(MERGE_TM // GMLP_BLOCK):
            rows = pl.ds(blk * GMLP_BLOCK, GMLP_BLOCK)
            mixed = jnp.dot(w, guv_ref[rows, vcols], preferred_element_type=F32) + bs_ref[g]
            gm_ref[rows, cols] = (guv_ref[rows, cols].astype(F32) * mixed).astype(BF16)

    d0 = jnp.dot(og_ref[...], wb0_ref[...], preferred_element_type=F32)
    d1 = jnp.dot(gm_ref[...], wb1_ref[...], preferred_element_type=F32)
    mixed = (gate_ref[:, :D_MODEL].astype(F32) * d0
             + gate_ref[:, D_MODEL:].astype(F32) * d1).astype(BF16)
    h_ref[...] = x_ref[...] + jnp.dot(mixed, wo_ref[...], preferred_element_type=F32)


def _merge(o_gla, p, x2, w_spatial, b_sp, wb0, wb1, wo):
    m = x2.shape[0]
    const2 = lambda i: (0, 0)
    const3 = lambda i: (0, 0, 0)
    single = pl.Buffered(1)
    return pl.pallas_call(
        _merge_kernel,
        grid=(m // MERGE_TM,),
        in_specs=[
            pl.BlockSpec((MERGE_TM, V_W), lambda i: (i, 0)),
            pl.BlockSpec((MERGE_TM, 2 * D_MODEL), lambda i: (i, COL_GU // (2 * D_MODEL))),
            pl.BlockSpec((MERGE_TM, 2 * D_MODEL), lambda i: (i, COL_GATE // (2 * D_MODEL))),
            pl.BlockSpec((MERGE_TM, D_MODEL), lambda i: (i, 0)),
            pl.BlockSpec((GMLP_GROUPS, GMLP_BLOCK, GMLP_BLOCK), const3, pipeline_mode=single),
            pl.BlockSpec((GMLP_GROUPS, GMLP_BLOCK, GMLP_DG), const3, pipeline_mode=single),
            pl.BlockSpec((V_W, D_MODEL), const2, pipeline_mode=single),
            pl.BlockSpec((V_W, D_MODEL), const2, pipeline_mode=single),
            pl.BlockSpec((D_MODEL, D_MODEL), const2, pipeline_mode=single),
        ],
        out_specs=pl.BlockSpec((MERGE_TM, D_MODEL), lambda i: (i, 0)),
        out_shape=jax.ShapeDtypeStruct((m, D_MODEL), F32),
        scratch_shapes=[pltpu.VMEM((MERGE_TM, D_MODEL), BF16)],
        compiler_params=pltpu.CompilerParams(
            dimension_semantics=("arbitrary",), vmem_limit_bytes=VMEM_LIMIT),
        name="merge",
    )(o_gla, p, p, x2, w_spatial, b_sp, wb0, wb1, wo)


def _ffn_kernel(h_ref, nw_ref, wu_ref, wd_ref, fw_ref, y_ref, hn_ref, acc_ref):
    f = pl.program_id(1)

    @pl.when(f == 0)
    def _():
        h = h_ref[...]
        ms = jnp.mean(h * h, axis=-1, keepdims=True)
        hn_ref[...] = (h * lax.rsqrt(ms + EPS) * nw_ref[...]).astype(BF16)
        acc_ref[...] = h

    up = jnp.dot(hn_ref[...], wu_ref[...], preferred_element_type=F32)
    up = jnp.square(jnp.maximum(up, 0.0)).astype(BF16)
    acc_ref[...] += jnp.dot(up, wd_ref[...], preferred_element_type=F32)

    @pl.when(f == pl.num_programs(1) - 1)
    def _():
        h2 = acc_ref[...]
        ms = jnp.mean(h2 * h2, axis=-1, keepdims=True)
        y_ref[...] = h2 * lax.rsqrt(ms + EPS) * fw_ref[...]


def _ffn(h, norm_w, wu, wd, final_w):
    m = h.shape[0]
    return pl.pallas_call(
        _ffn_kernel,
        grid=(m // FFN_TM, D_FF // FFN_TF),
        in_specs=[
            pl.BlockSpec((FFN_TM, D_MODEL), lambda i, f: (i, 0)),
            pl.BlockSpec((1, D_MODEL), lambda i, f: (0, 0)),
            pl.BlockSpec((D_MODEL, FFN_TF), lambda i, f: (0, f)),
            pl.BlockSpec((FFN_TF, D_MODEL), lambda i, f: (f, 0)),
            pl.BlockSpec((1, D_MODEL), lambda i, f: (0, 0)),
        ],
        out_specs=pl.BlockSpec((FFN_TM, D_MODEL), lambda i, f: (i, 0)),
        out_shape=jax.ShapeDtypeStruct((m, D_MODEL), F32),
        scratch_shapes=[
            pltpu.VMEM((FFN_TM, D_MODEL), BF16),
            pltpu.VMEM((FFN_TM, D_MODEL), F32),
        ],
        compiler_params=pltpu.CompilerParams(
            dimension_semantics=("arbitrary", "arbitrary"), vmem_limit_bytes=VMEM_LIMIT),
        name="ffn",
    )(h, norm_w, wu, wd, final_w)


def _layer(h2, batch, seq, norm_mix_w, w_in, w_alpha_up, b_alpha, gla_norm_w, gmlp_ln_w,
           gmlp_ln_b, w_spatial, b_spatial, b_gate, w_branch, w_out, norm_mlp_w, w_ff_up,
           w_ff_down, final_w):
    w_in_t = w_in.T
    xn, q, glr = _qnorm(h2, norm_mix_w.reshape(1, D_MODEL), w_in_t)
    p = _in_proj(xn, w_in_t, b_gate.reshape(1, 2 * D_MODEL), gmlp_ln_w.reshape(1, GMLP_DG),
                 gmlp_ln_b.reshape(1, GMLP_DG))
    o_gla, wb0_b, wb1_b, wo_b, wu_b, wd_b = _gla(
        q, p, glr, w_alpha_up.astype(BF16), b_alpha.reshape(1, QK_W),
        gla_norm_w.reshape(1, GLA_DV), w_branch, w_out, w_ff_up, w_ff_down, batch, seq)
    b_sp = jnp.broadcast_to(b_spatial[:, :, None], (GMLP_GROUPS, GMLP_BLOCK, GMLP_DG))
    h_mid = _merge(o_gla, p, h2, w_spatial, b_sp, wb0_b, wb1_b, wo_b)
    return _ffn(h_mid, norm_mlp_w.reshape(1, D_MODEL), wu_b, wd_b, final_w)


def kernel(x, norm_mix_w, w_in, w_alpha_up, b_alpha, gla_norm_w, gmlp_ln_w, gmlp_ln_b,
           w_spatial, b_spatial, b_gate, w_branch, w_out, norm_mlp_w, w_ff_up, w_ff_down,
           norm_final_w):
    batch, seq, d = x.shape
    depth = w_in.shape[0]
    assert d == D_MODEL and depth == 1 and seq % GLA_TB == 0
    assert w_in.shape[2] == D_IN
    h2 = x.reshape(batch * seq, d)
    y = _layer(h2, batch, seq, norm_mix_w[0], w_in[0], w_alpha_up[0], b_alpha[0],
               gla_norm_w[0], gmlp_ln_w[0], gmlp_ln_b[0], w_spatial[0], b_spatial[0],
               b_gate[0], w_branch[0], w_out[0], norm_mlp_w[0], w_ff_up[0], w_ff_down[0],
               norm_final_w.reshape(1, D_MODEL))
    return y.reshape(batch, seq, d)
```

```python
import math

import jax
import jax.numpy as jnp
from jax import lax
from jax.experimental import pallas as pl
from jax.experimental.pallas import tpu as pltpu

D_MODEL = 2048
CHUNK = 64
GLA_HEADS = 4
GLA_DK = D_MODEL // 8
GLA_DV = D_MODEL // 4
GLA_GATE_RANK = 16
GLA_TAU = 16.0
GMLP_BLOCK = 128
GMLP_GROUPS = 8
GMLP_DG = D_MODEL // GMLP_GROUPS
D_FF = 4 * D_MODEL
EPS = 1e-6
QK_W = GLA_HEADS * GLA_DK
V_W = GLA_HEADS * GLA_DV
D_IN = 2 * QK_W + 2 * V_W + GLA_GATE_RANK + 4 * D_MODEL

F32 = jnp.float32
BF16 = jnp.bfloat16

LANES = 128

WCOL_Q = 0
WCOL_K = QK_W
WCOL_V = 2 * QK_W
WCOL_R = WCOL_V + V_W
WCOL_GLR = WCOL_R + V_W
WCOL_GU = WCOL_GLR + GLA_GATE_RANK

COL_V = 0
COL_R = COL_V + V_W
COL_GU = COL_R + V_W
COL_GV = COL_GU + D_MODEL
COL_GATE = COL_GV + D_MODEL
COL_K = COL_GATE + 2 * D_MODEL
P_WIDTH = COL_K + QK_W

V7X_VMEM_BYTES = 64 * 1024 * 1024
VMEM_LIMIT = V7X_VMEM_BYTES - 8 * 1024 * 1024

QN_TM = 1024
QN_RC = 256
IN_TM = 2048
IN_TN = 1024
IN_RC = 512
IN_UNROLL = 2
CAST_RC = 256
GLA_TB = 256
GLA_UNROLL = 4
MERGE_TM = 256
FFN_TM = 512
FFN_TF = 1024


def _sigmoid(x):
    return 0.5 * (1.0 + jnp.tanh(0.5 * x))


def _gelu_bf16(x):
    inner = math.sqrt(2.0 / math.pi) * (x + 0.044715 * (x * x * x))
    return (0.5 * x.astype(BF16)) * (1.0 + jnp.tanh(inner.astype(BF16)))


def _cast_transposed(wt_ref, wb_ref, n_rows):
    for rc in range(n_rows // CAST_RC):
        rows = pl.ds(rc * CAST_RC, CAST_RC)
        wb_ref[:, rc * CAST_RC:(rc + 1) * CAST_RC] = wt_ref[rows, :].T.astype(BF16)


def _qnorm_kernel(x_ref, nw_ref, wq_ref, wg_ref, xn_ref, q_ref, glr_ref, wqg_ref):
    @pl.when(pl.program_id(0) == 0)
    def _():
        _cast_transposed(wq_ref, wqg_ref, QK_W)
        wqg_ref[:, QK_W:] = wg_ref[...].T.astype(BF16)

    for c in range(QN_TM // QN_RC):
        rows = pl.ds(c * QN_RC, QN_RC)
        x = x_ref[rows, :]
        ms = jnp.mean(x * x, axis=-1, keepdims=True)
        xn = (x * lax.rsqrt(ms + EPS) * nw_ref[...]).astype(BF16)
        xn_ref[rows, :] = xn
        qg = jnp.dot(xn, wqg_ref[...], preferred_element_type=F32)
        q_ref[rows, :] = (qg[:, :QK_W] * (GLA_DK ** -0.5)).astype(BF16)
        glr_ref[rows, :] = qg[:, QK_W:]


def _qnorm(x2, norm_w, w_in_t):
    m = x2.shape[0]
    single = pl.Buffered(1)
    return pl.pallas_call(
        _qnorm_kernel,
        grid=(m // QN_TM,),
        in_specs=[
            pl.BlockSpec((QN_TM, D_MODEL), lambda i: (i, 0)),
            pl.BlockSpec((1, D_MODEL), lambda i: (0, 0)),
            pl.BlockSpec((QK_W, D_MODEL), lambda i: (WCOL_Q // QK_W, 0), pipeline_mode=single),
            pl.BlockSpec((LANES, D_MODEL), lambda i: (WCOL_GLR // LANES, 0), pipeline_mode=single),
        ],
        out_specs=[
            pl.BlockSpec((QN_TM, D_MODEL), lambda i: (i, 0)),
            pl.BlockSpec((QN_TM, QK_W), lambda i: (i, 0)),
            pl.BlockSpec((QN_TM, LANES), lambda i: (i, 0)),
        ],
        out_shape=[
            jax.ShapeDtypeStruct((m, D_MODEL), BF16),
            jax.ShapeDtypeStruct((m, QK_W), BF16),
            jax.ShapeDtypeStruct((m, LANES), F32),
        ],
        scratch_shapes=[pltpu.VMEM((D_MODEL, QK_W + LANES), BF16)],
        compiler_params=pltpu.CompilerParams(
            dimension_semantics=("arbitrary",), vmem_limit_bytes=VMEM_LIMIT),
        name="qnorm",
    )(x2, norm_w, w_in_t, w_in_t)


_T_GU, _T_GV, _T_GATE, _T_K = (c // IN_TN for c in (COL_GU, COL_GV, COL_GATE, COL_K))
_N_T = P_WIDTH // IN_TN


def _in_proj_kernel(xn_ref, w_ref, bg_ref, lnw_ref, lnb_ref, p_ref, wb_ref):
    t = pl.program_id(0)
    i = pl.program_id(1)

    @pl.when(i == 0)
    def _():
        _cast_transposed(w_ref, wb_ref, IN_TN)

    def run(epilogue):
        def chunk_body(c, carry):
            rows = pl.ds(pl.multiple_of(c * IN_RC, IN_RC), IN_RC)
            acc = jnp.dot(xn_ref[rows, :], wb_ref[...], preferred_element_type=F32)
            p_ref[rows, :] = epilogue(acc).astype(BF16)
            return carry

        lax.fori_loop(0, IN_TM // IN_RC, chunk_body, 0, unroll=IN_UNROLL)

    def epi_ln(acc):
        outs = []
        for g in range(IN_TN // GMLP_DG):
            cols = slice(g * GMLP_DG, (g + 1) * GMLP_DG)
            v = _gelu_bf16(acc[:, cols]).astype(F32)
            mu = jnp.mean(v, axis=-1, keepdims=True)
            vc = v - mu
            var = jnp.mean(vc * vc, axis=-1, keepdims=True)
            outs.append(vc * lax.rsqrt(var + EPS) * lnw_ref[...] + lnb_ref[...])
        return jnp.concatenate(outs, axis=-1)

    @pl.when(jnp.logical_or(t < _T_GU, t >= _T_K))
    def _():
        run(lambda a: a)

    @pl.when(jnp.logical_and(t >= _T_GU, t < _T_GV))
    def _():
        run(_gelu_bf16)

    @pl.when(jnp.logical_and(t >= _T_GV, t < _T_GATE))
    def _():
        run(epi_ln)

    @pl.when(jnp.logical_and(t >= _T_GATE, t < _T_K))
    def _():
        run(lambda a: _sigmoid((a + bg_ref[...]).astype(BF16)))


def _in_proj(xn, w_in_t, b_gate, ln_w, ln_b):
    m = xn.shape[0]
    n_i = m // IN_TM
    n_gate_blocks = (2 * D_MODEL) // IN_TN

    def w_rows(t, i):
        col = jnp.where(t < _T_GU, WCOL_V + t * IN_TN,
                        jnp.where(t < _T_K, WCOL_GU + (t - _T_GU) * IN_TN, WCOL_K))
        return (pl.multiple_of(col, GLA_GATE_RANK), 0)

    return pl.pallas_call(
        _in_proj_kernel,
        grid=(_N_T, n_i),
        in_specs=[
            pl.BlockSpec((IN_TM, D_MODEL), lambda t, i: (i, 0)),
            pl.BlockSpec((pl.Element(IN_TN), pl.Element(D_MODEL)), w_rows),
            pl.BlockSpec((1, IN_TN),
                         lambda t, i: (0, jnp.clip(t - _T_GATE, 0, n_gate_blocks - 1))),
            pl.BlockSpec((1, GMLP_DG), lambda t, i: (0, 0)),
            pl.BlockSpec((1, GMLP_DG), lambda t, i: (0, 0)),
        ],
        out_specs=pl.BlockSpec((IN_TM, IN_TN), lambda t, i: (i, t)),
        out_shape=jax.ShapeDtypeStruct((m, P_WIDTH), BF16),
        scratch_shapes=[pltpu.VMEM((D_MODEL, IN_TN), BF16)],
        compiler_params=pltpu.CompilerParams(
            dimension_semantics=("arbitrary", "arbitrary"), vmem_limit_bytes=VMEM_LIMIT),
        name="in_proj",
    )(xn, w_in_t, b_gate, ln_w, ln_b)


def _gla_kernel(q_ref, k_ref, vr_ref, glr_ref, wup_ref, ba_ref, gnw_ref,
                wb0_ref, wb1_ref, wo_ref, wu_ref, wd_ref,
                o_ref, wb0b_ref, wb1b_ref, wob_ref, wub_ref, wdb_ref, state_ref, z_ref):
    t = pl.program_id(1)

    @pl.when(t == 0)
    def _():
        state_ref[...] = jnp.zeros_like(state_ref)

    glr = glr_ref[:, :GLA_GATE_RANK].astype(BF16)
    z_ref[...] = jnp.dot(glr, wup_ref[...], preferred_element_type=F32) + ba_ref[...]

    row = lax.broadcasted_iota(jnp.int32, (CHUNK, CHUNK), 0)
    col = lax.broadcasted_iota(jnp.int32, (CHUNK, CHUNK), 1)
    tri = (row >= col).astype(BF16)
    tri2 = jnp.concatenate([tri, tri], axis=1)

    n_chunks = GLA_TB // CHUNK
    w_rows = wb0_ref.shape[0] // n_chunks
    wu_rows = wu_ref.shape[0] // n_chunks
    wd_rows = wd_ref.shape[0] // n_chunks

    def chunk_body(c, carry):
        for src, dst, n in ((wb0_ref, wb0b_ref, w_rows), (wb1_ref, wb1b_ref, w_rows),
                            (wo_ref, wob_ref, w_rows), (wu_ref, wub_ref, wu_rows),
                            (wd_ref, wdb_ref, wd_rows)):
            wr = pl.ds(pl.multiple_of(c * n, n), n)
            dst[wr, :] = src[wr, :].astype(BF16)

        rows = pl.ds(pl.multiple_of(c * CHUNK, CHUNK), CHUNK)
        z = z_ref[rows, :]
        la = (jnp.minimum(z, 0.0) - jnp.log1p(jnp.exp(-jnp.abs(z)))) * (1.0 / GLA_TAU)
        la_hi = la.astype(BF16)
        la_lo = (la - la_hi.astype(F32)).astype(BF16)
        lcum = jnp.dot(tri2, jnp.concatenate([la_hi, la_lo], axis=0),
                       preferred_element_type=F32)
        l_end = lcum[CHUNK - 1:CHUNK, :]
        k_dec = (k_ref[rows, :].astype(F32) * jnp.exp(l_end - lcum)).astype(BF16)
        decay = jnp.exp(l_end)
        q = q_ref[rows, :]
        v = vr_ref[rows, :V_W]
        r = vr_ref[rows, V_W:]
        rs = r * _sigmoid(r)
        for h in range(GLA_HEADS):
            kc = slice(h * GLA_DK, (h + 1) * GLA_DK)
            vc = slice(h * GLA_DV, (h + 1) * GLA_DV)
            kv_t = lax.dot_general(v[:, vc], k_dec[:, kc], (((0,), (0,)), ((), ())),
                                   preferred_element_type=F32)
            s = state_ref[h] * decay[:, kc] + kv_t
            state_ref[h] = s
            o = lax.dot_general(q[:, kc], s.astype(BF16), (((1,), (1,)), ((), ())),
                                preferred_element_type=F32)
            ms = jnp.mean(o * o, axis=-1, keepdims=True)
            o = o * lax.rsqrt(ms + EPS) * gnw_ref[...] * rs[:, vc].astype(F32)
            o_ref[rows, vc] = o.astype(BF16)
        return carry

    lax.fori_loop(0, n_chunks, chunk_body, 0, unroll=GLA_UNROLL)


def _gla(q, p, glr, wup, b_alpha, gnw, w_branch, w_out, w_ff_up, w_ff_down, batch, seq):
    m = p.shape[0]
    nt = seq // GLA_TB
    n_steps = batch * nt
    rowblk = lambda b, t: b * nt + t
    w_spec = pl.BlockSpec((D_MODEL // n_steps, D_MODEL), lambda b, t: (rowblk(b, t), 0))
    wbr_spec = lambda n: pl.BlockSpec((None, D_MODEL // n_steps, D_MODEL),
                                      lambda b, t: (n, rowblk(b, t), 0))
    wu_spec = pl.BlockSpec((D_MODEL // n_steps, D_FF), lambda b, t: (rowblk(b, t), 0))
    wd_spec = pl.BlockSpec((D_FF // n_steps, D_MODEL), lambda b, t: (rowblk(b, t), 0))
    w_shape = jax.ShapeDtypeStruct((D_MODEL, D_MODEL), BF16)
    return pl.pallas_call(
        _gla_kernel,
        grid=(batch, nt),
        in_specs=[
            pl.BlockSpec((GLA_TB, QK_W), lambda b, t: (rowblk(b, t), 0)),
            pl.BlockSpec((GLA_TB, QK_W), lambda b, t: (rowblk(b, t), COL_K // QK_W)),
            pl.BlockSpec((GLA_TB, 2 * V_W), lambda b, t: (rowblk(b, t), COL_V // (2 * V_W))),
            pl.BlockSpec((GLA_TB, LANES), lambda b, t: (rowblk(b, t), 0)),
            pl.BlockSpec((GLA_GATE_RANK, QK_W), lambda b, t: (0, 0)),
            pl.BlockSpec((1, QK_W), lambda b, t: (0, 0)),
            pl.BlockSpec((1, GLA_DV), lambda b, t: (0, 0)),
            wbr_spec(0), wbr_spec(1), w_spec, wu_spec, wd_spec,
        ],
        out_specs=[pl.BlockSpec((GLA_TB, V_W), lambda b, t: (rowblk(b, t), 0)),
                   w_spec, w_spec, w_spec, wu_spec, wd_spec],
        out_shape=[jax.ShapeDtypeStruct((m, V_W), BF16), w_shape, w_shape, w_shape,
                   jax.ShapeDtypeStruct((D_MODEL, D_FF), BF16),
                   jax.ShapeDtypeStruct((D_FF, D_MODEL), BF16)],
        scratch_shapes=[
            pltpu.VMEM((GLA_HEADS, GLA_DV, GLA_DK), F32),
            pltpu.VMEM((GLA_TB, QK_W), F32),
        ],
        compiler_params=pltpu.CompilerParams(
            dimension_semantics=("arbitrary", "arbitrary"), vmem_limit_bytes=VMEM_LIMIT),
        name="gla",
    )(q, p, p, glr, wup, b_alpha, gnw, w_branch, w_branch, w_out, w_ff_up, w_ff_down)


def _merge_kernel(og_ref, guv_ref, gate_ref, x_ref, ws_ref, bs_ref,
                  wb0_ref, wb1_ref, wo_ref, h_ref, gm_ref):
    pos_t = lax.broadcasted_iota(jnp.int32, (GMLP_BLOCK, GMLP_BLOCK), 0) // CHUNK
    pos_s = lax.broadcasted_iota(jnp.int32, (GMLP_BLOCK, GMLP_BLOCK), 1) // CHUNK
    causal = pos_t >= pos_s
    for g in range(GMLP_GROUPS):
        w = jnp.where(causal, ws_ref[g], 0.0).astype(BF16)
        cols = slice(g * GMLP_DG, (g + 1) * GMLP_DG)
        vcols = slice(D_MODEL + g * GMLP_DG, D_MODEL + (g + 1) * GMLP_DG)
        for blk in range(MERGE_TM // GMLP_BLOCK):
            rows = pl.ds(blk * GMLP_BLOCK, GMLP_BLOCK)
            mixed = jnp.dot(w, guv_ref[rows, vcols], preferred_element_type=F32) + bs_ref[g]
            gm_ref[rows, cols] = (guv_ref[rows, cols].astype(F32) * mixed).astype(BF16)

    d0 = jnp.dot(og_ref[...], wb0_ref[...], preferred_element_type=F32)
    d1 = jnp.dot(gm_ref[...], wb1_ref[...], preferred_element_type=F32)
    mixed = (gate_ref[:, :D_MODEL].astype(F32) * d0
             + gate_ref[:, D_MODEL:].astype(F32) * d1).astype(BF16)
    h_ref[...] = x_ref[...] + jnp.dot(mixed, wo_ref[...], preferred_element_type=F32)


def _merge(o_gla, p, x2, w_spatial, b_sp, wb0, wb1, wo):
    m = x2.shape[0]
    const2 = lambda i: (0, 0)
    const3 = lambda i: (0, 0, 0)
    single = pl.Buffered(1)
    return pl.pallas_call(
        _merge_kernel,
        grid=(m // MERGE_TM,),
        in_specs=[
            pl.BlockSpec((MERGE_TM, V_W), lambda i: (i, 0)),
            pl.BlockSpec((MERGE_TM, 2 * D_MODEL), lambda i: (i, COL_GU // (2 * D_MODEL))),
            pl.BlockSpec((MERGE_TM, 2 * D_MODEL), lambda i: (i, COL_GATE // (2 * D_MODEL))),
            pl.BlockSpec((MERGE_TM, D_MODEL), lambda i: (i, 0)),
            pl.BlockSpec((GMLP_GROUPS, GMLP_BLOCK, GMLP_BLOCK), const3, pipeline_mode=single),
            pl.BlockSpec((GMLP_GROUPS, GMLP_BLOCK, GMLP_DG), const3, pipeline_mode=single),
            pl.BlockSpec((V_W, D_MODEL), const2, pipeline_mode=single),
            pl.BlockSpec((V_W, D_MODEL), const2, pipeline_mode=single),
            pl.BlockSpec((D_MODEL, D_MODEL), const2, pipeline_mode=single),
        ],
        out_specs=pl.BlockSpec((MERGE_TM, D_MODEL), lambda i: (i, 0)),
        out_shape=jax.ShapeDtypeStruct((m, D_MODEL), F32),
        scratch_shapes=[pltpu.VMEM((MERGE_TM, D_MODEL), BF16)],
        compiler_params=pltpu.CompilerParams(
            dimension_semantics=("arbitrary",), vmem_limit_bytes=VMEM_LIMIT),
        name="merge",
    )(o_gla, p, p, x2, w_spatial, b_sp, wb0, wb1, wo)


def _ffn_kernel(h_ref, nw_ref, wu_ref, wd_ref, fw_ref, y_ref, hn_ref):
    f = pl.program_id(1)

    @pl.when(f == 0)
    def _():
        h = h_ref[...]
        ms = jnp.mean(h * h, axis=-1, keepdims=True)
        hn_ref[...] = (h * lax.rsqrt(ms + EPS) * nw_ref[...]).astype(BF16)
        y_ref[...] = h

    up = jnp.dot(hn_ref[...], wu_ref[...], preferred_element_type=F32)
    up = jnp.square(jnp.maximum(up, 0.0)).astype(BF16)
    y_ref[...] += jnp.dot(up, wd_ref[...], preferred_element_type=F32)

    @pl.when(f == pl.num_programs(1) - 1)
    def _():
        h2 = y_ref[...]
        ms = jnp.mean(h2 * h2, axis=-1, keepdims=True)
        y_ref[...] = h2 * lax.rsqrt(ms + EPS) * fw_ref[...]


def _ffn(h, norm_w, wu, wd, final_w):
    m = h.shape[0]
    return pl.pallas_call(
        _ffn_kernel,
        grid=(m // FFN_TM, D_FF // FFN_TF),
        in_specs=[
            pl.BlockSpec((FFN_TM, D_MODEL), lambda i, f: (i, 0)),
            pl.BlockSpec((1, D_MODEL), lambda i, f: (0, 0)),
            pl.BlockSpec((D_MODEL, FFN_TF), lambda i, f: (0, f)),
            pl.BlockSpec((FFN_TF, D_MODEL), lambda i, f: (f, 0)),
            pl.BlockSpec((1, D_MODEL), lambda i, f: (0, 0)),
        ],
        out_specs=pl.BlockSpec((FFN_TM, D_MODEL), lambda i, f: (i, 0)),
        out_shape=jax.ShapeDtypeStruct((m, D_MODEL), F32),
        scratch_shapes=[pltpu.VMEM((FFN_TM, D_MODEL), BF16)],
        compiler_params=pltpu.CompilerParams(
            dimension_semantics=("arbitrary", "arbitrary"), vmem_limit_bytes=VMEM_LIMIT),
        name="ffn",
    )(h, norm_w, wu, wd, final_w)


def _layer(h2, batch, seq, norm_mix_w, w_in, w_alpha_up, b_alpha, gla_norm_w, gmlp_ln_w,
           gmlp_ln_b, w_spatial, b_spatial, b_gate, w_branch, w_out, norm_mlp_w, w_ff_up,
           w_ff_down, final_w):
    w_in_t = w_in.T
    xn, q, glr = _qnorm(h2, norm_mix_w.reshape(1, D_MODEL), w_in_t)
    p = _in_proj(xn, w_in_t, b_gate.reshape(1, 2 * D_MODEL), gmlp_ln_w.reshape(1, GMLP_DG),
                 gmlp_ln_b.reshape(1, GMLP_DG))
    o_gla, wb0_b, wb1_b, wo_b, wu_b, wd_b = _gla(
        q, p, glr, w_alpha_up.astype(BF16), b_alpha.reshape(1, QK_W),
        gla_norm_w.reshape(1, GLA_DV), w_branch, w_out, w_ff_up, w_ff_down, batch, seq)
    b_sp = jnp.broadcast_to(b_spatial[:, :, None], (GMLP_GROUPS, GMLP_BLOCK, GMLP_DG))
    h_mid = _merge(o_gla, p, h2, w_spatial, b_sp, wb0_b, wb1_b, wo_b)
    return _ffn(h_mid, norm_mlp_w.reshape(1, D_MODEL), wu_b, wd_b, final_w)


def kernel(x, norm_mix_w, w_in, w_alpha_up, b_alpha, gla_norm_w, gmlp_ln_w, gmlp_ln_b,
           w_spatial, b_spatial, b_gate, w_branch, w_out, norm_mlp_w, w_ff_up, w_ff_down,
           norm_final_w):
    batch, seq, d = x.shape
    depth = w_in.shape[0]
    assert d == D_MODEL and depth == 1 and seq % GLA_TB == 0
    assert w_in.shape[2] == D_IN
    h2 = x.reshape(batch * seq, d)
    y = _layer(h2, batch, seq, norm_mix_w[0], w_in[0], w_alpha_up[0], b_alpha[0],
               gla_norm_w[0], gmlp_ln_w[0], gmlp_ln_b[0], w_spatial[0], b_spatial[0],
               b_gate[0], w_branch[0], w_out[0], norm_mlp_w[0], w_ff_up[0], w_ff_down[0],
               norm_final_w.reshape(1, D_MODEL))
    return y.reshape(batch, seq, d)
```

```python
import math

import jax
import jax.numpy as jnp
from jax import lax
from jax.experimental import pallas as pl
from jax.experimental.pallas import tpu as pltpu

D_MODEL = 2048
CHUNK = 64
GLA_HEADS = 4
GLA_DK = D_MODEL // 8
GLA_DV = D_MODEL // 4
GLA_GATE_RANK = 16
GLA_TAU = 16.0
GMLP_BLOCK = 128
GMLP_GROUPS = 8
GMLP_DG = D_MODEL // GMLP_GROUPS
D_FF = 4 * D_MODEL
EPS = 1e-6
QK_W = GLA_HEADS * GLA_DK
V_W = GLA_HEADS * GLA_DV
D_IN = 2 * QK_W + 2 * V_W + GLA_GATE_RANK + 4 * D_MODEL

F32 = jnp.float32
BF16 = jnp.bfloat16

LANES = 128

WCOL_Q = 0
WCOL_K = QK_W
WCOL_V = 2 * QK_W
WCOL_R = WCOL_V + V_W
WCOL_GLR = WCOL_R + V_W
WCOL_GU = WCOL_GLR + GLA_GATE_RANK

COL_V = 0
COL_R = COL_V + V_W
COL_GU = COL_R + V_W
COL_GV = COL_GU + D_MODEL
COL_GATE = COL_GV + D_MODEL
COL_K = COL_GATE + 2 * D_MODEL
P_WIDTH = COL_K + QK_W

V7X_VMEM_BYTES = 64 * 1024 * 1024
VMEM_LIMIT = V7X_VMEM_BYTES - 8 * 1024 * 1024

QN_TM = 1024
QN_RC = 256
IN_TM = 2048
IN_TN = 1024
IN_RC = 512
IN_NC = 512
IN_UNROLL = 2
IN_SIDE_STEPS = 32
CAST_RC = 256
GLA_TB = 256
GLA_UNROLL = 4
MERGE_TM = 256
FFN_TM = 512
FFN_TF = 1024


def _sigmoid(x):
    return 0.5 * (1.0 + jnp.tanh(0.5 * x))


def _gelu_bf16(x):
    inner = math.sqrt(2.0 / math.pi) * (x + 0.044715 * (x * x * x))
    return (0.5 * x.astype(BF16)) * (1.0 + jnp.tanh(inner.astype(BF16)))


def _cast_transposed(wt_ref, wb_ref, n_rows):
    for rc in range(n_rows // CAST_RC):
        rows = pl.ds(rc * CAST_RC, CAST_RC)
        wb_ref[:, rc * CAST_RC:(rc + 1) * CAST_RC] = wt_ref[rows, :].T.astype(BF16)


def _qnorm_kernel(x_ref, nw_ref, wq_ref, wg_ref, xn_ref, q_ref, glr_ref, wqg_ref):
    @pl.when(pl.program_id(0) == 0)
    def _():
        _cast_transposed(wq_ref, wqg_ref, QK_W)
        wqg_ref[:, QK_W:] = wg_ref[...].T.astype(BF16)

    for c in range(QN_TM // QN_RC):
        rows = pl.ds(c * QN_RC, QN_RC)
        x = x_ref[rows, :]
        ms = jnp.mean(x * x, axis=-1, keepdims=True)
        xn = (x * lax.rsqrt(ms + EPS) * nw_ref[...]).astype(BF16)
        xn_ref[rows, :] = xn
        qg = jnp.dot(xn, wqg_ref[...], preferred_element_type=F32)
        q_ref[rows, :] = (qg[:, :QK_W] * (GLA_DK ** -0.5)).astype(BF16)
        glr_ref[rows, :] = qg[:, QK_W:]


def _qnorm(x2, norm_w, w_in_t):
    m = x2.shape[0]
    single = pl.Buffered(1)
    return pl.pallas_call(
        _qnorm_kernel,
        grid=(m // QN_TM,),
        in_specs=[
            pl.BlockSpec((QN_TM, D_MODEL), lambda i: (i, 0)),
            pl.BlockSpec((1, D_MODEL), lambda i: (0, 0)),
            pl.BlockSpec((QK_W, D_MODEL), lambda i: (WCOL_Q // QK_W, 0), pipeline_mode=single),
            pl.BlockSpec((LANES, D_MODEL), lambda i: (WCOL_GLR // LANES, 0), pipeline_mode=single),
        ],
        out_specs=[
            pl.BlockSpec((QN_TM, D_MODEL), lambda i: (i, 0)),
            pl.BlockSpec((QN_TM, QK_W), lambda i: (i, 0)),
            pl.BlockSpec((QN_TM, LANES), lambda i: (i, 0)),
        ],
        out_shape=[
            jax.ShapeDtypeStruct((m, D_MODEL), BF16),
            jax.ShapeDtypeStruct((m, QK_W), BF16),
            jax.ShapeDtypeStruct((m, LANES), F32),
        ],
        scratch_shapes=[pltpu.VMEM((D_MODEL, QK_W + LANES), BF16)],
        compiler_params=pltpu.CompilerParams(
            dimension_semantics=("arbitrary",), vmem_limit_bytes=VMEM_LIMIT),
        name="qnorm",
    )(x2, norm_w, w_in_t, w_in_t)


_T_GU, _T_GV, _T_GATE, _T_K = (c // IN_TN for c in (COL_GU, COL_GV, COL_GATE, COL_K))
_N_T = P_WIDTH // IN_TN


def _in_proj_kernel(xn_ref, w_ref, bg_ref, lnw_ref, lnb_ref, wb0_ref, wb1_ref, wo_ref,
                    p_ref, wb0b_ref, wb1b_ref, wob_ref, wb_ref):
    t = pl.program_id(0)
    i = pl.program_id(1)

    @pl.when(i == 0)
    def _():
        _cast_transposed(w_ref, wb_ref, IN_TN)

    side_rows = wo_ref.shape[0] // (IN_TM // IN_RC)

    def run(epilogue):
        def chunk_body(c, carry):
            rows = pl.ds(pl.multiple_of(c * IN_RC, IN_RC), IN_RC)
            for n in range(IN_TN // IN_NC):
                cols = slice(n * IN_NC, (n + 1) * IN_NC)
                acc = jnp.dot(xn_ref[rows, :], wb_ref[:, cols], preferred_element_type=F32)
                p_ref[rows, cols] = epilogue(acc, cols).astype(BF16)
            wr = pl.ds(pl.multiple_of(c * side_rows, side_rows), side_rows)
            for src, dst in ((wb0_ref, wb0b_ref), (wb1_ref, wb1b_ref), (wo_ref, wob_ref)):
                dst[wr, :] = src[wr, :].astype(BF16)
            return carry

        lax.fori_loop(0, IN_TM // IN_RC, chunk_body, 0, unroll=IN_UNROLL)

    def epi_ln(acc, _):
        outs = []
        for g in range(IN_NC // GMLP_DG):
            cols = slice(g * GMLP_DG, (g + 1) * GMLP_DG)
            v = _gelu_bf16(acc[:, cols]).astype(F32)
            mu = jnp.mean(v, axis=-1, keepdims=True)
            vc = v - mu
            var = jnp.mean(vc * vc, axis=-1, keepdims=True)
            outs.append(vc * lax.rsqrt(var + EPS) * lnw_ref[...] + lnb_ref[...])
        return jnp.concatenate(outs, axis=-1)

    @pl.when(jnp.logical_or(t < _T_GU, t >= _T_K))
    def _():
        run(lambda a, _: a)

    @pl.when(jnp.logical_and(t >= _T_GU, t < _T_GV))
    def _():
        run(lambda a, _: _gelu_bf16(a))

    @pl.when(jnp.logical_and(t >= _T_GV, t < _T_GATE))
    def _():
        run(epi_ln)

    @pl.when(jnp.logical_and(t >= _T_GATE, t < _T_K))
    def _():
        run(lambda a, cols: _sigmoid((a + bg_ref[:, cols]).astype(BF16)))


def _in_proj(xn, w_in_t, b_gate, ln_w, ln_b, w_branch, w_out):
    m = xn.shape[0]
    n_i = m // IN_TM
    n_gate_blocks = (2 * D_MODEL) // IN_TN
    assert _N_T * n_i >= IN_SIDE_STEPS
    side_blk = lambda t, i: jnp.minimum(t * n_i + i, IN_SIDE_STEPS - 1)
    side_spec = pl.BlockSpec((D_MODEL // IN_SIDE_STEPS, D_MODEL),
                             lambda t, i: (side_blk(t, i), 0))
    side_br_spec = lambda n: pl.BlockSpec((None, D_MODEL // IN_SIDE_STEPS, D_MODEL),
                                          lambda t, i: (n, side_blk(t, i), 0))
    w_shape = jax.ShapeDtypeStruct((D_MODEL, D_MODEL), BF16)

    def w_rows(t, i):
        col = jnp.where(t < _T_GU, WCOL_V + t * IN_TN,
                        jnp.where(t < _T_K, WCOL_GU + (t - _T_GU) * IN_TN, WCOL_K))
        return (pl.multiple_of(col, GLA_GATE_RANK), 0)

    return pl.pallas_call(
        _in_proj_kernel,
        grid=(_N_T, n_i),
        in_specs=[
            pl.BlockSpec((IN_TM, D_MODEL), lambda t, i: (i, 0)),
            pl.BlockSpec((pl.Element(IN_TN), pl.Element(D_MODEL)), w_rows),
            pl.BlockSpec((1, IN_TN),
                         lambda t, i: (0, jnp.clip(t - _T_GATE, 0, n_gate_blocks - 1))),
            pl.BlockSpec((1, GMLP_DG), lambda t, i: (0, 0)),
            pl.BlockSpec((1, GMLP_DG), lambda t, i: (0, 0)),
            side_br_spec(0), side_br_spec(1), side_spec,
        ],
        out_specs=[pl.BlockSpec((IN_TM, IN_TN), lambda t, i: (i, t)),
                   side_spec, side_spec, side_spec],
        out_shape=[jax.ShapeDtypeStruct((m, P_WIDTH), BF16), w_shape, w_shape, w_shape],
        scratch_shapes=[pltpu.VMEM((D_MODEL, IN_TN), BF16)],
        compiler_params=pltpu.CompilerParams(
            dimension_semantics=("arbitrary", "arbitrary"), vmem_limit_bytes=VMEM_LIMIT),
        name="in_proj",
    )(xn, w_in_t, b_gate, ln_w, ln_b, w_branch, w_branch, w_out)


def _gla_kernel(q_ref, k_ref, vr_ref, glr_ref, wup_ref, ba_ref, gnw_ref,
                wu_ref, wd_ref, o_ref, wub_ref, wdb_ref, state_ref, z_ref):
    t = pl.program_id(1)

    @pl.when(t == 0)
    def _():
        state_ref[...] = jnp.zeros_like(state_ref)

    glr = glr_ref[:, :GLA_GATE_RANK].astype(BF16)
    z_ref[...] = jnp.dot(glr, wup_ref[...], preferred_element_type=F32) + ba_ref[...]

    row = lax.broadcasted_iota(jnp.int32, (CHUNK, CHUNK), 0)
    col = lax.broadcasted_iota(jnp.int32, (CHUNK, CHUNK), 1)
    tri = (row >= col).astype(BF16)
    tri2 = jnp.concatenate([tri, tri], axis=1)

    n_chunks = GLA_TB // CHUNK
    wu_rows = wu_ref.shape[0] // n_chunks
    wd_rows = wd_ref.shape[0] // n_chunks

    def chunk_body(c, carry):
        for src, dst, n in ((wu_ref, wub_ref, wu_rows), (wd_ref, wdb_ref, wd_rows)):
            wr = pl.ds(pl.multiple_of(c * n, n), n)
            dst[wr, :] = src[wr, :].astype(BF16)

        rows = pl.ds(pl.multiple_of(c * CHUNK, CHUNK), CHUNK)
        z = z_ref[rows, :]
        la = (jnp.minimum(z, 0.0) - jnp.log(1.0 + jnp.exp(-jnp.abs(z)))) * (1.0 / GLA_TAU)
        la_hi = la.astype(BF16)
        la_lo = (la - la_hi.astype(F32)).astype(BF16)
        lcum = jnp.dot(tri2, jnp.concatenate([la_hi, la_lo], axis=0),
                       preferred_element_type=F32)
        l_end = lcum[CHUNK - 1:CHUNK, :]
        k_dec = (k_ref[rows, :].astype(F32) * jnp.exp(l_end - lcum)).astype(BF16)
        decay = jnp.exp(l_end)
        q = q_ref[rows, :]
        v = vr_ref[rows, :V_W]
        r = vr_ref[rows, V_W:]
        rs = r * _sigmoid(r)
        for h in range(GLA_HEADS):
            kc = slice(h * GLA_DK, (h + 1) * GLA_DK)
            vc = slice(h * GLA_DV, (h + 1) * GLA_DV)
            kv_t = lax.dot_general(v[:, vc], k_dec[:, kc], (((0,), (0,)), ((), ())),
                                   preferred_element_type=F32)
            s = state_ref[h] * decay[:, kc] + kv_t
            state_ref[h] = s
            o = lax.dot_general(q[:, kc], s.astype(BF16), (((1,), (1,)), ((), ())),
                                preferred_element_type=F32)
            ms = jnp.mean(o * o, axis=-1, keepdims=True)
            o = o * lax.rsqrt(ms + EPS) * gnw_ref[...] * rs[:, vc].astype(F32)
            o_ref[rows, vc] = o.astype(BF16)
        return carry

    lax.fori_loop(0, n_chunks, chunk_body, 0, unroll=GLA_UNROLL)


def _gla(q, p, glr, wup, b_alpha, gnw, w_ff_up, w_ff_down, batch, seq):
    m = p.shape[0]
    nt = seq // GLA_TB
    n_steps = batch * nt
    rowblk = lambda b, t: b * nt + t
    wu_spec = pl.BlockSpec((D_MODEL // n_steps, D_FF), lambda b, t: (rowblk(b, t), 0))
    wd_spec = pl.BlockSpec((D_FF // n_steps, D_MODEL), lambda b, t: (rowblk(b, t), 0))
    return pl.pallas_call(
        _gla_kernel,
        grid=(batch, nt),
        in_specs=[
            pl.BlockSpec((GLA_TB, QK_W), lambda b, t: (rowblk(b, t), 0)),
            pl.BlockSpec((GLA_TB, QK_W), lambda b, t: (rowblk(b, t), COL_K // QK_W)),
            pl.BlockSpec((GLA_TB, 2 * V_W), lambda b, t: (rowblk(b, t), COL_V // (2 * V_W))),
            pl.BlockSpec((GLA_TB, LANES), lambda b, t: (rowblk(b, t), 0)),
            pl.BlockSpec((GLA_GATE_RANK, QK_W), lambda b, t: (0, 0)),
            pl.BlockSpec((1, QK_W), lambda b, t: (0, 0)),
            pl.BlockSpec((1, GLA_DV), lambda b, t: (0, 0)),
            wu_spec, wd_spec,
        ],
        out_specs=[pl.BlockSpec((GLA_TB, V_W), lambda b, t: (rowblk(b, t), 0)),
                   wu_spec, wd_spec],
        out_shape=[jax.ShapeDtypeStruct((m, V_W), BF16),
                   jax.ShapeDtypeStruct((D_MODEL, D_FF), BF16),
                   jax.ShapeDtypeStruct((D_FF, D_MODEL), BF16)],
        scratch_shapes=[
            pltpu.VMEM((GLA_HEADS, GLA_DV, GLA_DK), F32),
            pltpu.VMEM((GLA_TB, QK_W), F32),
        ],
        compiler_params=pltpu.CompilerParams(
            dimension_semantics=("arbitrary", "arbitrary"), vmem_limit_bytes=VMEM_LIMIT),
        name="gla",
    )(q, p, p, glr, wup, b_alpha, gnw, w_ff_up, w_ff_down)


def _merge_kernel(og_ref, guv_ref, gate_ref, x_ref, ws_ref, bs_ref,
                  wb0_ref, wb1_ref, wo_ref, h_ref, gm_ref):
    pos_t = lax.broadcasted_iota(jnp.int32, (GMLP_BLOCK, GMLP_BLOCK), 0) // CHUNK
    pos_s = lax.broadcasted_iota(jnp.int32, (GMLP_BLOCK, GMLP_BLOCK), 1) // CHUNK
    causal = pos_t >= pos_s
    for g in range(GMLP_GROUPS):
        w = jnp.where(causal, ws_ref[g], 0.0).astype(BF16)
        cols = slice(g * GMLP_DG, (g + 1) * GMLP_DG)
        vcols = slice(D_MODEL + g * GMLP_DG, D_MODEL + (g + 1) * GMLP_DG)
        for blk in range(MERGE_TM // GMLP_BLOCK):
            rows = pl.ds(blk * GMLP_BLOCK, GMLP_BLOCK)
            mixed = jnp.dot(w, guv_ref[rows, vcols], preferred_element_type=F32) + bs_ref[g]
            gm_ref[rows, cols] = (guv_ref[rows, cols].astype(F32) * mixed).astype(BF16)

    d0 = jnp.dot(og_ref[...], wb0_ref[...], preferred_element_type=F32)
    d1 = jnp.dot(gm_ref[...], wb1_ref[...], preferred_element_type=F32)
    mixed = (gate_ref[:, :D_MODEL].astype(F32) * d0
             + gate_ref[:, D_MODEL:].astype(F32) * d1).astype(BF16)
    h_ref[...] = x_ref[...] + jnp.dot(mixed, wo_ref[...], preferred_element_type=F32)


def _merge(o_gla, p, x2, w_spatial, b_sp, wb0, wb1, wo):
    m = x2.shape[0]
    const2 = lambda i: (0, 0)
    const3 = lambda i: (0, 0, 0)
    single = pl.Buffered(1)
    return pl.pallas_call(
        _merge_kernel,
        grid=(m // MERGE_TM,),
        in_specs=[
            pl.BlockSpec((MERGE_TM, V_W), lambda i: (i, 0)),
            pl.BlockSpec((MERGE_TM, 2 * D_MODEL), lambda i: (i, COL_GU // (2 * D_MODEL))),
            pl.BlockSpec((MERGE_TM, 2 * D_MODEL), lambda i: (i, COL_GATE // (2 * D_MODEL))),
            pl.BlockSpec((MERGE_TM, D_MODEL), lambda i: (i, 0)),
            pl.BlockSpec((GMLP_GROUPS, GMLP_BLOCK, GMLP_BLOCK), const3, pipeline_mode=single),
            pl.BlockSpec((GMLP_GROUPS, GMLP_BLOCK, GMLP_DG), const3, pipeline_mode=single),
            pl.BlockSpec((V_W, D_MODEL), const2, pipeline_mode=single),
            pl.BlockSpec((V_W, D_MODEL), const2, pipeline_mode=single),
            pl.BlockSpec((D_MODEL, D_MODEL), const2, pipeline_mode=single),
        ],
        out_specs=pl.BlockSpec((MERGE_TM, D_MODEL), lambda i: (i, 0)),
        out_shape=jax.ShapeDtypeStruct((m, D_MODEL), F32),
        scratch_shapes=[pltpu.VMEM((MERGE_TM, D_MODEL), BF16)],
        compiler_params=pltpu.CompilerParams(
            dimension_semantics=("arbitrary",), vmem_limit_bytes=VMEM_LIMIT),
        name="merge",
    )(o_gla, p, p, x2, w_spatial, b_sp, wb0, wb1, wo)


def _ffn_kernel(h_ref, nw_ref, wu_ref, wd_ref, fw_ref, y_ref, hn_ref):
    f = pl.program_id(1)

    @pl.when(f == 0)
    def _():
        h = h_ref[...]
        ms = jnp.mean(h * h, axis=-1, keepdims=True)
        hn_ref[...] = (h * lax.rsqrt(ms + EPS) * nw_ref[...]).astype(BF16)
        y_ref[...] = h

    up = jnp.dot(hn_ref[...], wu_ref[...], preferred_element_type=F32)
    up = jnp.square(jnp.maximum(up, 0.0)).astype(BF16)
    y_ref[...] += jnp.dot(up, wd_ref[...], preferred_element_type=F32)

    @pl.when(f == pl.num_programs(1) - 1)
    def _():
        h2 = y_ref[...]
        ms = jnp.mean(h2 * h2, axis=-1, keepdims=True)
        y_ref[...] = h2 * lax.rsqrt(ms + EPS) * fw_ref[...]


def _ffn(h, norm_w, wu, wd, final_w):
    m = h.shape[0]
    return pl.pallas_call(
        _ffn_kernel,
        grid=(m // FFN_TM, D_FF // FFN_TF),
        in_specs=[
            pl.BlockSpec((FFN_TM, D_MODEL), lambda i, f: (i, 0)),
            pl.BlockSpec((1, D_MODEL), lambda i, f: (0, 0)),
            pl.BlockSpec((D_MODEL, FFN_TF), lambda i, f: (0, f)),
            pl.BlockSpec((FFN_TF, D_MODEL), lambda i, f: (f, 0)),
            pl.BlockSpec((1, D_MODEL), lambda i, f: (0, 0)),
        ],
        out_specs=pl.BlockSpec((FFN_TM, D_MODEL), lambda i, f: (i, 0)),
        out_shape=jax.ShapeDtypeStruct((m, D_MODEL), F32),
        scratch_shapes=[pltpu.VMEM((FFN_TM, D_MODEL), BF16)],
        compiler_params=pltpu.CompilerParams(
            dimension_semantics=("arbitrary", "arbitrary"), vmem_limit_bytes=VMEM_LIMIT),
        name="ffn",
    )(h, norm_w, wu, wd, final_w)


def _layer(h2, batch, seq, norm_mix_w, w_in, w_alpha_up, b_alpha, gla_norm_w, gmlp_ln_w,
           gmlp_ln_b, w_spatial, b_spatial, b_gate, w_branch, w_out, norm_mlp_w, w_ff_up,
           w_ff_down, final_w):
    w_in_t = w_in.T
    xn, q, glr = _qnorm(h2, norm_mix_w.reshape(1, D_MODEL), w_in_t)
    p, wb0_b, wb1_b, wo_b = _in_proj(
        xn, w_in_t, b_gate.reshape(1, 2 * D_MODEL), gmlp_ln_w.reshape(1, GMLP_DG),
        gmlp_ln_b.reshape(1, GMLP_DG), w_branch, w_out)
    o_gla, wu_b, wd_b = _gla(
        q, p, glr, w_alpha_up.astype(BF16), b_alpha.reshape(1, QK_W),
        gla_norm_w.reshape(1, GLA_DV), w_ff_up, w_ff_down, batch, seq)
    b_sp = jnp.broadcast_to(b_spatial[:, :, None], (GMLP_GROUPS, GMLP_BLOCK, GMLP_DG))
    h_mid = _merge(o_gla, p, h2, w_spatial, b_sp, wb0_b, wb1_b, wo_b)
    return _ffn(h_mid, norm_mlp_w.reshape(1, D_MODEL), wu_b, wd_b, final_w)


def kernel(x, norm_mix_w, w_in, w_alpha_up, b_alpha, gla_norm_w, gmlp_ln_w, gmlp_ln_b,
           w_spatial, b_spatial, b_gate, w_branch, w_out, norm_mlp_w, w_ff_up, w_ff_down,
           norm_final_w):
    batch, seq, d = x.shape
    depth = w_in.shape[0]
    assert d == D_MODEL and depth == 1 and seq % GLA_TB == 0
    assert w_in.shape[2] == D_IN
    h2 = x.reshape(batch * seq, d)
    y = _layer(h2, batch, seq, norm_mix_w[0], w_in[0], w_alpha_up[0], b_alpha[0],
               gla_norm_w[0], gmlp_ln_w[0], gmlp_ln_b[0], w_spatial[0], b_spatial[0],
               b_gate[0], w_branch[0], w_out[0], norm_mlp_w[0], w_ff_up[0], w_ff_down[0],
               norm_final_w.reshape(1, D_MODEL))
    return y.reshape(batch, seq, d)
```

```python
import math

import jax
import jax.numpy as jnp
from jax import lax
from jax.experimental import pallas as pl
from jax.experimental.pallas import tpu as pltpu

D_MODEL = 2048
CHUNK = 64
GLA_HEADS = 4
GLA_DK = D_MODEL // 8
GLA_DV = D_MODEL // 4
GLA_GATE_RANK = 16
GLA_TAU = 16.0
GMLP_BLOCK = 128
GMLP_GROUPS = 8
GMLP_DG = D_MODEL // GMLP_GROUPS
D_FF = 4 * D_MODEL
EPS = 1e-6
QK_W = GLA_HEADS * GLA_DK
V_W = GLA_HEADS * GLA_DV
D_IN = 2 * QK_W + 2 * V_W + GLA_GATE_RANK + 4 * D_MODEL

F32 = jnp.float32
BF16 = jnp.bfloat16

LANES = 128

WCOL_Q = 0
WCOL_K = QK_W
WCOL_V = 2 * QK_W
WCOL_R = WCOL_V + V_W
WCOL_GLR = WCOL_R + V_W
WCOL_GU = WCOL_GLR + GLA_GATE_RANK

COL_V = 0
COL_R = COL_V + V_W
COL_GU = COL_R + V_W
COL_GV = COL_GU + D_MODEL
COL_GATE = COL_GV + D_MODEL
COL_K = COL_GATE + 2 * D_MODEL
P_WIDTH = COL_K + QK_W

V7X_VMEM_BYTES = 64 * 1024 * 1024
VMEM_LIMIT = V7X_VMEM_BYTES - 8 * 1024 * 1024

QN_TM = 1024
QN_RC = 256
IN_TM = 2048
IN_TN = 1024
IN_RC = 512
IN_NC = 512
IN_UNROLL = 2
IN_SIDE_STEPS = 32
CAST_RC = 256
GLA_TB = 256
GLA_UNROLL = 4
MERGE_TM = 256
FFN_TM = 512
FFN_TF = 1024


def _sigmoid(x):
    return 0.5 * (1.0 + jnp.tanh(0.5 * x))


def _gelu_bf16(x):
    c = math.sqrt(2.0 / math.pi)
    inner = x * (c + (c * 0.044715) * (x * x))
    return (0.5 * x.astype(BF16)) * (1.0 + jnp.tanh(inner.astype(BF16)))


def _cast_transposed(wt_ref, wb_ref, n_rows):
    for rc in range(n_rows // CAST_RC):
        rows = pl.ds(rc * CAST_RC, CAST_RC)
        wb_ref[:, rc * CAST_RC:(rc + 1) * CAST_RC] = wt_ref[rows, :].T.astype(BF16)


def _qnorm_kernel(x_ref, nw_ref, wq_ref, wg_ref, xn_ref, q_ref, glr_ref, wqg_ref):
    @pl.when(pl.program_id(0) == 0)
    def _():
        _cast_transposed(wq_ref, wqg_ref, QK_W)
        wqg_ref[:, QK_W:] = wg_ref[...].T.astype(BF16)

    for c in range(QN_TM // QN_RC):
        rows = pl.ds(c * QN_RC, QN_RC)
        x = x_ref[rows, :]
        ms = jnp.mean(x * x, axis=-1, keepdims=True)
        xn = (x * lax.rsqrt(ms + EPS) * nw_ref[...]).astype(BF16)
        xn_ref[rows, :] = xn
        qg = jnp.dot(xn, wqg_ref[...], preferred_element_type=F32)
        q_ref[rows, :] = (qg[:, :QK_W] * (GLA_DK ** -0.5)).astype(BF16)
        glr_ref[rows, :] = qg[:, QK_W:].astype(BF16)


def _qnorm(x2, norm_w, w_in_t):
    m = x2.shape[0]
    single = pl.Buffered(1)
    return pl.pallas_call(
        _qnorm_kernel,
        grid=(m // QN_TM,),
        in_specs=[
            pl.BlockSpec((QN_TM, D_MODEL), lambda i: (i, 0)),
            pl.BlockSpec((1, D_MODEL), lambda i: (0, 0)),
            pl.BlockSpec((QK_W, D_MODEL), lambda i: (WCOL_Q // QK_W, 0), pipeline_mode=single),
            pl.BlockSpec((LANES, D_MODEL), lambda i: (WCOL_GLR // LANES, 0), pipeline_mode=single),
        ],
        out_specs=[
            pl.BlockSpec((QN_TM, D_MODEL), lambda i: (i, 0)),
            pl.BlockSpec((QN_TM, QK_W), lambda i: (i, 0)),
            pl.BlockSpec((QN_TM, LANES), lambda i: (i, 0)),
        ],
        out_shape=[
            jax.ShapeDtypeStruct((m, D_MODEL), BF16),
            jax.ShapeDtypeStruct((m, QK_W), BF16),
            jax.ShapeDtypeStruct((m, LANES), BF16),
        ],
        scratch_shapes=[pltpu.VMEM((D_MODEL, QK_W + LANES), BF16)],
        compiler_params=pltpu.CompilerParams(
            dimension_semantics=("arbitrary",), vmem_limit_bytes=VMEM_LIMIT),
        name="qnorm",
    )(x2, norm_w, w_in_t, w_in_t)


_T_GU, _T_GV, _T_GATE, _T_K = (c // IN_TN for c in (COL_GU, COL_GV, COL_GATE, COL_K))
_N_T = P_WIDTH // IN_TN


def _in_proj_kernel(xn_ref, w_ref, bg_ref, lnw_ref, lnb_ref, wb0_ref, wb1_ref, wo_ref,
                    p_ref, wb0b_ref, wb1b_ref, wob_ref, wb_ref):
    t = pl.program_id(0)
    i = pl.program_id(1)

    @pl.when(i == 0)
    def _():
        _cast_transposed(w_ref, wb_ref, IN_TN)

    side_rows = wo_ref.shape[0] // (IN_TM // IN_RC)

    def run(epilogue):
        def chunk_body(c, carry):
            rows = pl.ds(pl.multiple_of(c * IN_RC, IN_RC), IN_RC)
            for n in range(IN_TN // IN_NC):
                cols = slice(n * IN_NC, (n + 1) * IN_NC)
                acc = jnp.dot(xn_ref[rows, :], wb_ref[:, cols], preferred_element_type=F32)
                p_ref[rows, cols] = epilogue(acc, cols).astype(BF16)
            wr = pl.ds(pl.multiple_of(c * side_rows, side_rows), side_rows)
            for src, dst in ((wb0_ref, wb0b_ref), (wb1_ref, wb1b_ref), (wo_ref, wob_ref)):
                dst[wr, :] = src[wr, :].astype(BF16)
            return carry

        lax.fori_loop(0, IN_TM // IN_RC, chunk_body, 0, unroll=IN_UNROLL)

    def epi_ln(acc, _):
        outs = []
        for g in range(IN_NC // GMLP_DG):
            cols = slice(g * GMLP_DG, (g + 1) * GMLP_DG)
            v = _gelu_bf16(acc[:, cols]).astype(F32)
            mu = jnp.mean(v, axis=-1, keepdims=True)
            vc = v - mu
            var = jnp.mean(vc * vc, axis=-1, keepdims=True)
            outs.append(vc * lax.rsqrt(var + EPS) * lnw_ref[...] + lnb_ref[...])
        return jnp.concatenate(outs, axis=-1)

    @pl.when(jnp.logical_or(t < _T_GU, t >= _T_K))
    def _():
        run(lambda a, _: a)

    @pl.when(jnp.logical_and(t >= _T_GU, t < _T_GV))
    def _():
        run(lambda a, _: _gelu_bf16(a))

    @pl.when(jnp.logical_and(t >= _T_GV, t < _T_GATE))
    def _():
        run(epi_ln)

    @pl.when(jnp.logical_and(t >= _T_GATE, t < _T_K))
    def _():
        branch = (t - _T_GATE) // (D_MODEL // IN_TN)
        run(lambda a, cols: _sigmoid((a + bg_ref[pl.ds(branch, 1), cols]).astype(BF16)))


def _in_proj(xn, w_in_t, b_gate, ln_w, ln_b, w_branch, w_out):
    m = xn.shape[0]
    n_i = m // IN_TM
    n_branch = b_gate.shape[0]
    n_col = D_MODEL // IN_TN
    n_gate_blocks = n_branch * n_col
    assert _N_T * n_i >= IN_SIDE_STEPS
    side_blk = lambda t, i: jnp.minimum(t * n_i + i, IN_SIDE_STEPS - 1)
    side_spec = pl.BlockSpec((D_MODEL // IN_SIDE_STEPS, D_MODEL),
                             lambda t, i: (side_blk(t, i), 0))
    side_br_spec = lambda n: pl.BlockSpec((None, D_MODEL // IN_SIDE_STEPS, D_MODEL),
                                          lambda t, i: (n, side_blk(t, i), 0))
    w_shape = jax.ShapeDtypeStruct((D_MODEL, D_MODEL), BF16)

    def w_rows(t, i):
        col = jnp.where(t < _T_GU, WCOL_V + t * IN_TN,
                        jnp.where(t < _T_K, WCOL_GU + (t - _T_GU) * IN_TN, WCOL_K))
        return (pl.multiple_of(col, GLA_GATE_RANK), 0)

    return pl.pallas_call(
        _in_proj_kernel,
        grid=(_N_T, n_i),
        in_specs=[
            pl.BlockSpec((IN_TM, D_MODEL), lambda t, i: (i, 0)),
            pl.BlockSpec((pl.Element(IN_TN), pl.Element(D_MODEL)), w_rows),
            pl.BlockSpec((n_branch, IN_TN),
                         lambda t, i: (0, jnp.clip(t - _T_GATE, 0, n_gate_blocks - 1) % n_col)),
            pl.BlockSpec((1, GMLP_DG), lambda t, i: (0, 0)),
            pl.BlockSpec((1, GMLP_DG), lambda t, i: (0, 0)),
            side_br_spec(0), side_br_spec(1), side_spec,
        ],
        out_specs=[pl.BlockSpec((IN_TM, IN_TN), lambda t, i: (i, t)),
                   side_spec, side_spec, side_spec],
        out_shape=[jax.ShapeDtypeStruct((m, P_WIDTH), BF16), w_shape, w_shape, w_shape],
        scratch_shapes=[pltpu.VMEM((D_MODEL, IN_TN), BF16)],
        compiler_params=pltpu.CompilerParams(
            dimension_semantics=("arbitrary", "arbitrary"), vmem_limit_bytes=VMEM_LIMIT),
        name="in_proj",
    )(xn, w_in_t, b_gate, ln_w, ln_b, w_branch, w_branch, w_out)


def _gla_kernel(q_ref, k_ref, vr_ref, glr_ref, wup_ref, ba_ref, gnw_ref,
                wu_ref, wd_ref, o_ref, wub_ref, wdb_ref, state_ref, z_ref):
    t = pl.program_id(1)

    @pl.when(t == 0)
    def _():
        state_ref[...] = jnp.zeros_like(state_ref)

    z_ref[...] = jnp.dot(glr_ref[:, :GLA_GATE_RANK], wup_ref[...].astype(BF16),
                         preferred_element_type=F32) + ba_ref[...]

    row = lax.broadcasted_iota(jnp.int32, (CHUNK, CHUNK), 0)
    col = lax.broadcasted_iota(jnp.int32, (CHUNK, CHUNK), 1)
    tri = (row >= col).astype(BF16)
    tri2 = jnp.concatenate([tri, tri], axis=1)

    n_chunks = GLA_TB // CHUNK
    wu_rows = wu_ref.shape[0] // n_chunks
    wd_rows = wd_ref.shape[0] // n_chunks

    def chunk_body(c, carry):
        for src, dst, n in ((wu_ref, wub_ref, wu_rows), (wd_ref, wdb_ref, wd_rows)):
            wr = pl.ds(pl.multiple_of(c * n, n), n)
            dst[wr, :] = src[wr, :].astype(BF16)

        rows = pl.ds(pl.multiple_of(c * CHUNK, CHUNK), CHUNK)
        z = z_ref[rows, :]
        la = (jnp.minimum(z, 0.0) - jnp.log(1.0 + jnp.exp(-jnp.abs(z)))) * (1.0 / GLA_TAU)
        la_hi = la.astype(BF16)
        la_lo = (la - la_hi.astype(F32)).astype(BF16)
        lcum = jnp.dot(tri2, jnp.concatenate([la_hi, la_lo], axis=0),
                       preferred_element_type=F32)
        l_end = lcum[CHUNK - 1:CHUNK, :]
        k_dec = (k_ref[rows, :].astype(F32) * jnp.exp(l_end - lcum)).astype(BF16)
        decay = jnp.exp(l_end)
        q = q_ref[rows, :]
        v = vr_ref[rows, :V_W]
        r = vr_ref[rows, V_W:]
        rs = r * _sigmoid(r)
        for h in range(GLA_HEADS):
            kc = slice(h * GLA_DK, (h + 1) * GLA_DK)
            vc = slice(h * GLA_DV, (h + 1) * GLA_DV)
            kv_t = lax.dot_general(v[:, vc], k_dec[:, kc], (((0,), (0,)), ((), ())),
                                   preferred_element_type=F32)
            s = state_ref[h] * decay[:, kc] + kv_t
            state_ref[h] = s
            o = lax.dot_general(q[:, kc], s.astype(BF16), (((1,), (1,)), ((), ())),
                                preferred_element_type=F32)
            ms = jnp.mean(o * o, axis=-1, keepdims=True)
            o = o * lax.rsqrt(ms + EPS) * gnw_ref[...] * rs[:, vc].astype(F32)
            o_ref[rows, vc] = o.astype(BF16)
        return carry

    lax.fori_loop(0, n_chunks, chunk_body, 0, unroll=GLA_UNROLL)


def _gla(q, p, glr, wup, b_alpha, gnw, w_ff_up, w_ff_down, batch, seq):
    m = p.shape[0]
    nt = seq // GLA_TB
    n_steps = batch * nt
    rowblk = lambda b, t: b * nt + t
    wu_spec = pl.BlockSpec((D_MODEL // n_steps, D_FF), lambda b, t: (rowblk(b, t), 0))
    wd_spec = pl.BlockSpec((D_FF // n_steps, D_MODEL), lambda b, t: (rowblk(b, t), 0))
    return pl.pallas_call(
        _gla_kernel,
        grid=(batch, nt),
        in_specs=[
            pl.BlockSpec((GLA_TB, QK_W), lambda b, t: (rowblk(b, t), 0)),
            pl.BlockSpec((GLA_TB, QK_W), lambda b, t: (rowblk(b, t), COL_K // QK_W)),
            pl.BlockSpec((GLA_TB, 2 * V_W), lambda b, t: (rowblk(b, t), COL_V // (2 * V_W))),
            pl.BlockSpec((GLA_TB, LANES), lambda b, t: (rowblk(b, t), 0)),
            pl.BlockSpec((GLA_GATE_RANK, QK_W), lambda b, t: (0, 0)),
            pl.BlockSpec((1, QK_W), lambda b, t: (0, 0)),
            pl.BlockSpec((1, GLA_DV), lambda b, t: (0, 0)),
            wu_spec, wd_spec,
        ],
        out_specs=[pl.BlockSpec((GLA_TB, V_W), lambda b, t: (rowblk(b, t), 0)),
                   wu_spec, wd_spec],
        out_shape=[jax.ShapeDtypeStruct((m, V_W), BF16),
                   jax.ShapeDtypeStruct((D_MODEL, D_FF), BF16),
                   jax.ShapeDtypeStruct((D_FF, D_MODEL), BF16)],
        scratch_shapes=[
            pltpu.VMEM((GLA_HEADS, GLA_DV, GLA_DK), F32),
            pltpu.VMEM((GLA_TB, QK_W), F32),
        ],
        compiler_params=pltpu.CompilerParams(
            dimension_semantics=("arbitrary", "arbitrary"), vmem_limit_bytes=VMEM_LIMIT),
        name="gla",
    )(q, p, p, glr, wup, b_alpha, gnw, w_ff_up, w_ff_down)


def _merge_kernel(og_ref, guv_ref, gate_ref, x_ref, ws_ref, bs_ref,
                  wb0_ref, wb1_ref, wo_ref, h_ref, gm_ref):
    pos_t = lax.broadcasted_iota(jnp.int32, (GMLP_BLOCK, GMLP_BLOCK), 0) // CHUNK
    pos_s = lax.broadcasted_iota(jnp.int32, (GMLP_BLOCK, GMLP_BLOCK), 1) // CHUNK
    causal = pos_t >= pos_s
    for g in range(GMLP_GROUPS):
        w = jnp.where(causal, ws_ref[g], 0.0).astype(BF16)
        cols = slice(g * GMLP_DG, (g + 1) * GMLP_DG)
        vcols = slice(D_MODEL + g * GMLP_DG, D_MODEL + (g + 1) * GMLP_DG)
        for blk in range(MERGE_TM // GMLP_BLOCK):
            rows = pl.ds(blk * GMLP_BLOCK, GMLP_BLOCK)
            mixed = jnp.dot(w, guv_ref[rows, vcols], preferred_element_type=F32) + bs_ref[g]
            gm_ref[rows, cols] = (guv_ref[rows, cols].astype(F32) * mixed).astype(BF16)

    d0 = jnp.dot(og_ref[...], wb0_ref[...], preferred_element_type=F32)
    d1 = jnp.dot(gm_ref[...], wb1_ref[...], preferred_element_type=F32)
    mixed = (gate_ref[:, :D_MODEL].astype(F32) * d0
             + gate_ref[:, D_MODEL:].astype(F32) * d1).astype(BF16)
    h_ref[...] = x_ref[...] + jnp.dot(mixed, wo_ref[...], preferred_element_type=F32)


def _merge(o_gla, p, x2, w_spatial, b_sp, wb0, wb1, wo):
    m = x2.shape[0]
    const2 = lambda i: (0, 0)
    const3 = lambda i: (0, 0, 0)
    single = pl.Buffered(1)
    return pl.pallas_call(
        _merge_kernel,
        grid=(m // MERGE_TM,),
        in_specs=[
            pl.BlockSpec((MERGE_TM, V_W), lambda i: (i, 0)),
            pl.BlockSpec((MERGE_TM, 2 * D_MODEL), lambda i: (i, COL_GU // (2 * D_MODEL))),
            pl.BlockSpec((MERGE_TM, 2 * D_MODEL), lambda i: (i, COL_GATE // (2 * D_MODEL))),
            pl.BlockSpec((MERGE_TM, D_MODEL), lambda i: (i, 0)),
            pl.BlockSpec((GMLP_GROUPS, GMLP_BLOCK, GMLP_BLOCK), const3, pipeline_mode=single),
            pl.BlockSpec((GMLP_GROUPS, GMLP_BLOCK, GMLP_DG), const3, pipeline_mode=single),
            pl.BlockSpec((V_W, D_MODEL), const2, pipeline_mode=single),
            pl.BlockSpec((V_W, D_MODEL), const2, pipeline_mode=single),
            pl.BlockSpec((D_MODEL, D_MODEL), const2, pipeline_mode=single),
        ],
        out_specs=pl.BlockSpec((MERGE_TM, D_MODEL), lambda i: (i, 0)),
        out_shape=jax.ShapeDtypeStruct((m, D_MODEL), F32),
        scratch_shapes=[pltpu.VMEM((MERGE_TM, D_MODEL), BF16)],
        compiler_params=pltpu.CompilerParams(
            dimension_semantics=("arbitrary",), vmem_limit_bytes=VMEM_LIMIT),
        name="merge",
    )(o_gla, p, p, x2, w_spatial, b_sp, wb0, wb1, wo)


def _ffn_kernel(h_ref, nw_ref, wu_ref, wd_ref, fw_ref, y_ref, hn_ref):
    f = pl.program_id(1)

    @pl.when(f == 0)
    def _():
        h = h_ref[...]
        ms = jnp.mean(h * h, axis=-1, keepdims=True)
        hn_ref[...] = (h * lax.rsqrt(ms + EPS) * nw_ref[...]).astype(BF16)
        y_ref[...] = h

    up = jnp.dot(hn_ref[...], wu_ref[...], preferred_element_type=F32)
    up = jnp.square(jnp.maximum(up, 0.0)).astype(BF16)
    y_ref[...] += jnp.dot(up, wd_ref[...], preferred_element_type=F32)

    @pl.when(f == pl.num_programs(1) - 1)
    def _():
        h2 = y_ref[...]
        ms = jnp.mean(h2 * h2, axis=-1, keepdims=True)
        y_ref[...] = h2 * lax.rsqrt(ms + EPS) * fw_ref[...]


def _ffn(h, norm_w, wu, wd, final_w):
    m = h.shape[0]
    return pl.pallas_call(
        _ffn_kernel,
        grid=(m // FFN_TM, D_FF // FFN_TF),
        in_specs=[
            pl.BlockSpec((FFN_TM, D_MODEL), lambda i, f: (i, 0)),
            pl.BlockSpec((1, D_MODEL), lambda i, f: (0, 0)),
            pl.BlockSpec((D_MODEL, FFN_TF), lambda i, f: (0, f)),
            pl.BlockSpec((FFN_TF, D_MODEL), lambda i, f: (f, 0)),
            pl.BlockSpec((1, D_MODEL), lambda i, f: (0, 0)),
        ],
        out_specs=pl.BlockSpec((FFN_TM, D_MODEL), lambda i, f: (i, 0)),
        out_shape=jax.ShapeDtypeStruct((m, D_MODEL), F32),
        scratch_shapes=[pltpu.VMEM((FFN_TM, D_MODEL), BF16)],
        compiler_params=pltpu.CompilerParams(
            dimension_semantics=("arbitrary", "arbitrary"), vmem_limit_bytes=VMEM_LIMIT),
        name="ffn",
    )(h, norm_w, wu, wd, final_w)


def _layer(h2, batch, seq, norm_mix_w, w_in, w_alpha_up, b_alpha, gla_norm_w, gmlp_ln_w,
           gmlp_ln_b, w_spatial, b_spatial, b_gate, w_branch, w_out, norm_mlp_w, w_ff_up,
           w_ff_down, final_w):
    w_in_t = w_in.T
    xn, q, glr = _qnorm(h2, norm_mix_w.reshape(1, D_MODEL), w_in_t)
    p, wb0_b, wb1_b, wo_b = _in_proj(
        xn, w_in_t, b_gate, gmlp_ln_w.reshape(1, GMLP_DG),
        gmlp_ln_b.reshape(1, GMLP_DG), w_branch, w_out)
    o_gla, wu_b, wd_b = _gla(
        q, p, glr, w_alpha_up, b_alpha.reshape(1, QK_W),
        gla_norm_w.reshape(1, GLA_DV), w_ff_up, w_ff_down, batch, seq)
    b_sp = jnp.broadcast_to(b_spatial[:, :, None], (GMLP_GROUPS, GMLP_BLOCK, GMLP_DG))
    h_mid = _merge(o_gla, p, h2, w_spatial, b_sp, wb0_b, wb1_b, wo_b)
    return _ffn(h_mid, norm_mlp_w.reshape(1, D_MODEL), wu_b, wd_b, final_w)


def kernel(x, norm_mix_w, w_in, w_alpha_up, b_alpha, gla_norm_w, gmlp_ln_w, gmlp_ln_b,
           w_spatial, b_spatial, b_gate, w_branch, w_out, norm_mlp_w, w_ff_up, w_ff_down,
           norm_final_w):
    batch, seq, d = x.shape
    depth = w_in.shape[0]
    assert d == D_MODEL and depth == 1 and seq % GLA_TB == 0
    assert w_in.shape[2] == D_IN
    h2 = x.reshape(batch * seq, d)
    y = _layer(h2, batch, seq, norm_mix_w[0], w_in[0], w_alpha_up[0], b_alpha[0],
               gla_norm_w[0], gmlp_ln_w[0], gmlp_ln_b[0], w_spatial[0], b_spatial[0],
               b_gate[0], w_branch[0], w_out[0], norm_mlp_w[0], w_ff_up[0], w_ff_down[0],
               norm_final_w.reshape(1, D_MODEL))
    return y.reshape(batch, seq, d)
```

```python
import math

import jax
import jax.numpy as jnp
from jax import lax
from jax.experimental import pallas as pl
from jax.experimental.pallas import tpu as pltpu

D_MODEL = 2048
CHUNK = 64
GLA_HEADS = 4
GLA_DK = D_MODEL // 8
GLA_DV = D_MODEL // 4
GLA_GATE_RANK = 16
GLA_TAU = 16.0
GMLP_BLOCK = 128
GMLP_GROUPS = 8
GMLP_DG = D_MODEL // GMLP_GROUPS
D_FF = 4 * D_MODEL
EPS = 1e-6
QK_W = GLA_HEADS * GLA_DK
V_W = GLA_HEADS * GLA_DV
D_IN = 2 * QK_W + 2 * V_W + GLA_GATE_RANK + 4 * D_MODEL

F32 = jnp.float32
BF16 = jnp.bfloat16

LANES = 128

WCOL_Q = 0
WCOL_K = QK_W
WCOL_V = 2 * QK_W
WCOL_R = WCOL_V + V_W
WCOL_GLR = WCOL_R + V_W
WCOL_GU = WCOL_GLR + GLA_GATE_RANK

COL_V = 0
COL_R = COL_V + V_W
COL_GU = COL_R + V_W
COL_GV = COL_GU + D_MODEL
COL_GATE = COL_GV + D_MODEL
COL_K = COL_GATE + 2 * D_MODEL
P_WIDTH = COL_K + QK_W

V7X_VMEM_BYTES = 64 * 1024 * 1024
VMEM_LIMIT = V7X_VMEM_BYTES - 8 * 1024 * 1024

QN_TM = 1024
QN_RC = 256
QN_SLOTS = 3
IN_TM = 2048
IN_TN = 1024
IN_RC = 512
IN_NC = 512
IN_UNROLL = 2
IN_SIDE_STEPS = 32
CAST_RC = 256
GLA_TB = 256
GLA_UNROLL = 4
MERGE_TM = 256
FFN_TM = 512
FFN_TF = 1024


def _sigmoid(x):
    return 0.5 * (1.0 + jnp.tanh(0.5 * x))


def _gelu_bf16(x):
    c = math.sqrt(2.0 / math.pi)
    inner = x * (c + (c * 0.044715) * (x * x))
    return (0.5 * x.astype(BF16)) * (1.0 + jnp.tanh(inner.astype(BF16)))


def _cast_transposed(wt_ref, wb_ref, n_rows):
    for rc in range(n_rows // CAST_RC):
        rows = pl.ds(rc * CAST_RC, CAST_RC)
        wb_ref[:, rc * CAST_RC:(rc + 1) * CAST_RC] = wt_ref[rows, :].T.astype(BF16)


def _qnorm_kernel(x_hbm, nw_ref, wq_ref, wg_ref, xn_ref, q_ref, glr_ref, wqg_ref, xbuf_ref, sem):
    i = pl.program_id(0)
    n = pl.num_programs(0)

    def x_copy(step, slot):
        return pltpu.make_async_copy(x_hbm.at[pl.ds(step * QN_TM, QN_TM), :],
                                     xbuf_ref.at[slot], sem.at[slot])

    @pl.when(i == 0)
    def _():
        for s in range(QN_SLOTS - 1):
            x_copy(s, s).start()

    ahead = i + (QN_SLOTS - 1)

    @pl.when(ahead < n)
    def _():
        x_copy(ahead, ahead % QN_SLOTS).start()

    @pl.when(i == 0)
    def _():
        _cast_transposed(wq_ref, wqg_ref, QK_W)
        wqg_ref[:, QK_W:] = wg_ref[...].T.astype(BF16)

    slot = i % QN_SLOTS
    x_copy(i, slot).wait()
    x_ref = xbuf_ref.at[slot]

    for c in range(QN_TM // QN_RC):
        rows = pl.ds(c * QN_RC, QN_RC)
        x = x_ref[rows, :]
        ms = jnp.mean(x * x, axis=-1, keepdims=True)
        xn = (x * lax.rsqrt(ms + EPS) * nw_ref[...]).astype(BF16)
        xn_ref[rows, :] = xn
        qg = jnp.dot(xn, wqg_ref[...], preferred_element_type=F32)
        q_ref[rows, :] = (qg[:, :QK_W] * (GLA_DK ** -0.5)).astype(BF16)
        glr_ref[rows, :] = qg[:, QK_W:].astype(BF16)


def _qnorm(x2, norm_w, w_in_t):
    m = x2.shape[0]
    single = pl.Buffered(1)
    return pl.pallas_call(
        _qnorm_kernel,
        grid=(m // QN_TM,),
        in_specs=[
            pl.BlockSpec(memory_space=pl.ANY),
            pl.BlockSpec((1, D_MODEL), lambda i: (0, 0)),
            pl.BlockSpec((QK_W, D_MODEL), lambda i: (WCOL_Q // QK_W, 0), pipeline_mode=single),
            pl.BlockSpec((LANES, D_MODEL), lambda i: (WCOL_GLR // LANES, 0), pipeline_mode=single),
        ],
        out_specs=[
            pl.BlockSpec((QN_TM, D_MODEL), lambda i: (i, 0)),
            pl.BlockSpec((QN_TM, QK_W), lambda i: (i, 0)),
            pl.BlockSpec((QN_TM, LANES), lambda i: (i, 0)),
        ],
        out_shape=[
            jax.ShapeDtypeStruct((m, D_MODEL), BF16),
            jax.ShapeDtypeStruct((m, QK_W), BF16),
            jax.ShapeDtypeStruct((m, LANES), BF16),
        ],
        scratch_shapes=[pltpu.VMEM((D_MODEL, QK_W + LANES), BF16),
                        pltpu.VMEM((QN_SLOTS, QN_TM, D_MODEL), F32),
                        pltpu.SemaphoreType.DMA((QN_SLOTS,))],
        compiler_params=pltpu.CompilerParams(
            dimension_semantics=("arbitrary",), vmem_limit_bytes=VMEM_LIMIT),
        name="qnorm",
    )(x2, norm_w, w_in_t, w_in_t)


_T_GU, _T_GV, _T_GATE, _T_K = (c // IN_TN for c in (COL_GU, COL_GV, COL_GATE, COL_K))
_N_T = P_WIDTH // IN_TN


def _in_proj_kernel(xn_ref, w_ref, bg_ref, lnw_ref, lnb_ref, wb0_ref, wb1_ref, wo_ref,
                    p_ref, wb0b_ref, wb1b_ref, wob_ref, wb_ref):
    t = pl.program_id(0)
    i = pl.program_id(1)

    @pl.when(i == 0)
    def _():
        _cast_transposed(w_ref, wb_ref, IN_TN)

    side_rows = wo_ref.shape[0] // (IN_TM // IN_RC)

    def run(epilogue):
        def chunk_body(c, carry):
            rows = pl.ds(pl.multiple_of(c * IN_RC, IN_RC), IN_RC)
            for n in range(IN_TN // IN_NC):
                cols = slice(n * IN_NC, (n + 1) * IN_NC)
                acc = jnp.dot(xn_ref[rows, :], wb_ref[:, cols], preferred_element_type=F32)
                p_ref[rows, cols] = epilogue(acc, cols).astype(BF16)
            wr = pl.ds(pl.multiple_of(c * side_rows, side_rows), side_rows)
            for src, dst in ((wb0_ref, wb0b_ref), (wb1_ref, wb1b_ref), (wo_ref, wob_ref)):
                dst[wr, :] = src[wr, :].astype(BF16)
            return carry

        lax.fori_loop(0, IN_TM // IN_RC, chunk_body, 0, unroll=IN_UNROLL)

    def epi_ln(acc, _):
        outs = []
        for g in range(IN_NC // GMLP_DG):
            cols = slice(g * GMLP_DG, (g + 1) * GMLP_DG)
            v = _gelu_bf16(acc[:, cols]).astype(F32)
            mu = jnp.mean(v, axis=-1, keepdims=True)
            vc = v - mu
            var = jnp.mean(vc * vc, axis=-1, keepdims=True)
            outs.append(vc * lax.rsqrt(var + EPS) * lnw_ref[...] + lnb_ref[...])
        return jnp.concatenate(outs, axis=-1)

    @pl.when(jnp.logical_or(t < _T_GU, t >= _T_K))
    def _():
        run(lambda a, _: a)

    @pl.when(jnp.logical_and(t >= _T_GU, t < _T_GV))
    def _():
        run(lambda a, _: _gelu_bf16(a))

    @pl.when(jnp.logical_and(t >= _T_GV, t < _T_GATE))
    def _():
        run(epi_ln)

    @pl.when(jnp.logical_and(t >= _T_GATE, t < _T_K))
    def _():
        branch = (t - _T_GATE) // (D_MODEL // IN_TN)
        run(lambda a, cols: _sigmoid((a + bg_ref[pl.ds(branch, 1), cols]).astype(BF16)))


def _in_proj(xn, w_in_t, b_gate, ln_w, ln_b, w_branch, w_out):
    m = xn.shape[0]
    n_i = m // IN_TM
    n_branch = b_gate.shape[0]
    n_col = D_MODEL // IN_TN
    n_gate_blocks = n_branch * n_col
    assert _N_T * n_i >= IN_SIDE_STEPS
    side_blk = lambda t, i: jnp.minimum(t * n_i + i, IN_SIDE_STEPS - 1)
    side_spec = pl.BlockSpec((D_MODEL // IN_SIDE_STEPS, D_MODEL),
                             lambda t, i: (side_blk(t, i), 0))
    side_br_spec = lambda n: pl.BlockSpec((None, D_MODEL // IN_SIDE_STEPS, D_MODEL),
                                          lambda t, i: (n, side_blk(t, i), 0))
    w_shape = jax.ShapeDtypeStruct((D_MODEL, D_MODEL), BF16)

    def w_rows(t, i):
        col = jnp.where(t < _T_GU, WCOL_V + t * IN_TN,
                        jnp.where(t < _T_K, WCOL_GU + (t - _T_GU) * IN_TN, WCOL_K))
        return (pl.multiple_of(col, GLA_GATE_RANK), 0)

    return pl.pallas_call(
        _in_proj_kernel,
        grid=(_N_T, n_i),
        in_specs=[
            pl.BlockSpec((IN_TM, D_MODEL), lambda t, i: (i, 0)),
            pl.BlockSpec((pl.Element(IN_TN), pl.Element(D_MODEL)), w_rows),
            pl.BlockSpec((n_branch, IN_TN),
                         lambda t, i: (0, jnp.clip(t - _T_GATE, 0, n_gate_blocks - 1) % n_col)),
            pl.BlockSpec((1, GMLP_DG), lambda t, i: (0, 0)),
            pl.BlockSpec((1, GMLP_DG), lambda t, i: (0, 0)),
            side_br_spec(0), side_br_spec(1), side_spec,
        ],
        out_specs=[pl.BlockSpec((IN_TM, IN_TN), lambda t, i: (i, t)),
                   side_spec, side_spec, side_spec],
        out_shape=[jax.ShapeDtypeStruct((m, P_WIDTH), BF16), w_shape, w_shape, w_shape],
        scratch_shapes=[pltpu.VMEM((D_MODEL, IN_TN), BF16)],
        compiler_params=pltpu.CompilerParams(
            dimension_semantics=("arbitrary", "arbitrary"), vmem_limit_bytes=VMEM_LIMIT),
        name="in_proj",
    )(xn, w_in_t, b_gate, ln_w, ln_b, w_branch, w_branch, w_out)


def _gla_kernel(q_ref, k_ref, vr_ref, glr_ref, wup_ref, ba_ref, gnw_ref,
                wu_ref, wd_ref, o_ref, wub_ref, wdb_ref, state_ref, z_ref):
    t = pl.program_id(1)

    @pl.when(t == 0)
    def _():
        state_ref[...] = jnp.zeros_like(state_ref)

    z_ref[...] = jnp.dot(glr_ref[:, :GLA_GATE_RANK], wup_ref[...].astype(BF16),
                         preferred_element_type=F32) + ba_ref[...]

    row = lax.broadcasted_iota(jnp.int32, (CHUNK, CHUNK), 0)
    col = lax.broadcasted_iota(jnp.int32, (CHUNK, CHUNK), 1)
    tri = (row >= col).astype(BF16)
    tri2 = jnp.concatenate([tri, tri], axis=1)

    n_chunks = GLA_TB // CHUNK
    wu_rows = wu_ref.shape[0] // n_chunks
    wd_rows = wd_ref.shape[0] // n_chunks

    def chunk_body(c, carry):
        for src, dst, n in ((wu_ref, wub_ref, wu_rows), (wd_ref, wdb_ref, wd_rows)):
            wr = pl.ds(pl.multiple_of(c * n, n), n)
            dst[wr, :] = src[wr, :].astype(BF16)

        rows = pl.ds(pl.multiple_of(c * CHUNK, CHUNK), CHUNK)
        z = z_ref[rows, :]
        la = (jnp.minimum(z, 0.0) - jnp.log(1.0 + jnp.exp(-jnp.abs(z)))) * (1.0 / GLA_TAU)
        la_hi = la.astype(BF16)
        la_lo = (la - la_hi.astype(F32)).astype(BF16)
        lcum = jnp.dot(tri2, jnp.concatenate([la_hi, la_lo], axis=0),
                       preferred_element_type=F32)
        l_end = lcum[CHUNK - 1:CHUNK, :]
        k_dec = (k_ref[rows, :].astype(F32) * jnp.exp(l_end - lcum)).astype(BF16)
        decay = jnp.exp(l_end)
        q = q_ref[rows, :]
        v = vr_ref[rows, :V_W]
        r = vr_ref[rows, V_W:]
        rs = r * _sigmoid(r)
        for h in range(GLA_HEADS):
            kc = slice(h * GLA_DK, (h + 1) * GLA_DK)
            vc = slice(h * GLA_DV, (h + 1) * GLA_DV)
            kv_t = lax.dot_general(v[:, vc], k_dec[:, kc], (((0,), (0,)), ((), ())),
                                   preferred_element_type=F32)
            s = state_ref[h] * decay[:, kc] + kv_t
            state_ref[h] = s
            o = lax.dot_general(q[:, kc], s.astype(BF16), (((1,), (1,)), ((), ())),
                                preferred_element_type=F32)
            ms = jnp.mean(o * o, axis=-1, keepdims=True)
            o = o * lax.rsqrt(ms + EPS) * gnw_ref[...] * rs[:, vc].astype(F32)
            o_ref[rows, vc] = o.astype(BF16)
        return carry

    lax.fori_loop(0, n_chunks, chunk_body, 0, unroll=GLA_UNROLL)


def _gla(q, p, glr, wup, b_alpha, gnw, w_ff_up, w_ff_down, batch, seq):
    m = p.shape[0]
    nt = seq // GLA_TB
    n_steps = batch * nt
    rowblk = lambda b, t: b * nt + t
    wu_spec = pl.BlockSpec((D_MODEL // n_steps, D_FF), lambda b, t: (rowblk(b, t), 0))
    wd_spec = pl.BlockSpec((D_FF // n_steps, D_MODEL), lambda b, t: (rowblk(b, t), 0))
    return pl.pallas_call(
        _gla_kernel,
        grid=(batch, nt),
        in_specs=[
            pl.BlockSpec((GLA_TB, QK_W), lambda b, t: (rowblk(b, t), 0)),
            pl.BlockSpec((GLA_TB, QK_W), lambda b, t: (rowblk(b, t), COL_K // QK_W)),
            pl.BlockSpec((GLA_TB, 2 * V_W), lambda b, t: (rowblk(b, t), COL_V // (2 * V_W))),
            pl.BlockSpec((GLA_TB, LANES), lambda b, t: (rowblk(b, t), 0)),
            pl.BlockSpec((GLA_GATE_RANK, QK_W), lambda b, t: (0, 0)),
            pl.BlockSpec((1, QK_W), lambda b, t: (0, 0)),
            pl.BlockSpec((1, GLA_DV), lambda b, t: (0, 0)),
            wu_spec, wd_spec,
        ],
        out_specs=[pl.BlockSpec((GLA_TB, V_W), lambda b, t: (rowblk(b, t), 0)),
                   wu_spec, wd_spec],
        out_shape=[jax.ShapeDtypeStruct((m, V_W), BF16),
                   jax.ShapeDtypeStruct((D_MODEL, D_FF), BF16),
                   jax.ShapeDtypeStruct((D_FF, D_MODEL), BF16)],
        scratch_shapes=[
            pltpu.VMEM((GLA_HEADS, GLA_DV, GLA_DK), F32),
            pltpu.VMEM((GLA_TB, QK_W), F32),
        ],
        compiler_params=pltpu.CompilerParams(
            dimension_semantics=("arbitrary", "arbitrary"), vmem_limit_bytes=VMEM_LIMIT),
        name="gla",
    )(q, p, p, glr, wup, b_alpha, gnw, w_ff_up, w_ff_down)


def _merge_kernel(og_ref, guv_ref, gate_ref, x_ref, ws_ref, bs_ref,
                  wb0_ref, wb1_ref, wo_ref, h_ref, gm_ref):
    pos_t = lax.broadcasted_iota(jnp.int32, (GMLP_BLOCK, GMLP_BLOCK), 0) // CHUNK
    pos_s = lax.broadcasted_iota(jnp.int32, (GMLP_BLOCK, GMLP_BLOCK), 1) // CHUNK
    causal = pos_t >= pos_s
    for g in range(GMLP_GROUPS):
        w = jnp.where(causal, ws_ref[g], 0.0).astype(BF16)
        cols = slice(g * GMLP_DG, (g + 1) * GMLP_DG)
        vcols = slice(D_MODEL + g * GMLP_DG, D_MODEL + (g + 1) * GMLP_DG)
        for blk in range(MERGE_TM // GMLP_BLOCK):
            rows = pl.ds(blk * GMLP_BLOCK, GMLP_BLOCK)
            mixed = jnp.dot(w, guv_ref[rows, vcols], preferred_element_type=F32) + bs_ref[g]
            gm_ref[rows, cols] = (guv_ref[rows, cols].astype(F32) * mixed).astype(BF16)

    d0 = jnp.dot(og_ref[...], wb0_ref[...], preferred_element_type=F32)
    d1 = jnp.dot(gm_ref[...], wb1_ref[...], preferred_element_type=F32)
    mixed = (gate_ref[:, :D_MODEL].astype(F32) * d0
             + gate_ref[:, D_MODEL:].astype(F32) * d1).astype(BF16)
    h_ref[...] = x_ref[...] + jnp.dot(mixed, wo_ref[...], preferred_element_type=F32)


def _merge(o_gla, p, x2, w_spatial, b_sp, wb0, wb1, wo):
    m = x2.shape[0]
    const2 = lambda i: (0, 0)
    const3 = lambda i: (0, 0, 0)
    single = pl.Buffered(1)
    return pl.pallas_call(
        _merge_kernel,
        grid=(m // MERGE_TM,),
        in_specs=[
            pl.BlockSpec((MERGE_TM, V_W), lambda i: (i, 0)),
            pl.BlockSpec((MERGE_TM, 2 * D_MODEL), lambda i: (i, COL_GU // (2 * D_MODEL))),
            pl.BlockSpec((MERGE_TM, 2 * D_MODEL), lambda i: (i, COL_GATE // (2 * D_MODEL))),
            pl.BlockSpec((MERGE_TM, D_MODEL), lambda i: (i, 0)),
            pl.BlockSpec((GMLP_GROUPS, GMLP_BLOCK, GMLP_BLOCK), const3, pipeline_mode=single),
            pl.BlockSpec((GMLP_GROUPS, GMLP_BLOCK, GMLP_DG), const3, pipeline_mode=single),
            pl.BlockSpec((V_W, D_MODEL), const2, pipeline_mode=single),
            pl.BlockSpec((V_W, D_MODEL), const2, pipeline_mode=single),
            pl.BlockSpec((D_MODEL, D_MODEL), const2, pipeline_mode=single),
        ],
        out_specs=pl.BlockSpec((MERGE_TM, D_MODEL), lambda i: (i, 0)),
        out_shape=jax.ShapeDtypeStruct((m, D_MODEL), F32),
        scratch_shapes=[pltpu.VMEM((MERGE_TM, D_MODEL), BF16)],
        compiler_params=pltpu.CompilerParams(
            dimension_semantics=("arbitrary",), vmem_limit_bytes=VMEM_LIMIT),
        name="merge",
    )(o_gla, p, p, x2, w_spatial, b_sp, wb0, wb1, wo)


def _ffn_kernel(h_ref, nw_ref, wu_ref, wd_ref, fw_ref, y_ref, hn_ref):
    f = pl.program_id(1)

    @pl.when(f == 0)
    def _():
        h = h_ref[...]
        ms = jnp.mean(h * h, axis=-1, keepdims=True)
        hn_ref[...] = (h * lax.rsqrt(ms + EPS) * nw_ref[...]).astype(BF16)
        y_ref[...] = h

    up = jnp.dot(hn_ref[...], wu_ref[...], preferred_element_type=F32)
    up = jnp.square(jnp.maximum(up, 0.0)).astype(BF16)
    y_ref[...] += jnp.dot(up, wd_ref[...], preferred_element_type=F32)

    @pl.when(f == pl.num_programs(1) - 1)
    def _():
        h2 = y_ref[...]
        ms = jnp.mean(h2 * h2, axis=-1, keepdims=True)
        y_ref[...] = h2 * lax.rsqrt(ms + EPS) * fw_ref[...]


def _ffn(h, norm_w, wu, wd, final_w):
    m = h.shape[0]
    return pl.pallas_call(
        _ffn_kernel,
        grid=(m // FFN_TM, D_FF // FFN_TF),
        in_specs=[
            pl.BlockSpec((FFN_TM, D_MODEL), lambda i, f: (i, 0)),
            pl.BlockSpec((1, D_MODEL), lambda i, f: (0, 0)),
            pl.BlockSpec((D_MODEL, FFN_TF), lambda i, f: (0, f)),
            pl.BlockSpec((FFN_TF, D_MODEL), lambda i, f: (f, 0)),
            pl.BlockSpec((1, D_MODEL), lambda i, f: (0, 0)),
        ],
        out_specs=pl.BlockSpec((FFN_TM, D_MODEL), lambda i, f: (i, 0)),
        out_shape=jax.ShapeDtypeStruct((m, D_MODEL), F32),
        scratch_shapes=[pltpu.VMEM((FFN_TM, D_MODEL), BF16)],
        compiler_params=pltpu.CompilerParams(
            dimension_semantics=("arbitrary", "arbitrary"), vmem_limit_bytes=VMEM_LIMIT),
        name="ffn",
    )(h, norm_w, wu, wd, final_w)


def _layer(h2, batch, seq, norm_mix_w, w_in, w_alpha_up, b_alpha, gla_norm_w, gmlp_ln_w,
           gmlp_ln_b, w_spatial, b_spatial, b_gate, w_branch, w_out, norm_mlp_w, w_ff_up,
           w_ff_down, final_w):
    w_in_t = w_in.T
    xn, q, glr = _qnorm(h2, norm_mix_w.reshape(1, D_MODEL), w_in_t)
    p, wb0_b, wb1_b, wo_b = _in_proj(
        xn, w_in_t, b_gate, gmlp_ln_w.reshape(1, GMLP_DG),
        gmlp_ln_b.reshape(1, GMLP_DG), w_branch, w_out)
    o_gla, wu_b, wd_b = _gla(
        q, p, glr, w_alpha_up, b_alpha.reshape(1, QK_W),
        gla_norm_w.reshape(1, GLA_DV), w_ff_up, w_ff_down, batch, seq)
    b_sp = jnp.broadcast_to(b_spatial[:, :, None], (GMLP_GROUPS, GMLP_BLOCK, GMLP_DG))
    h_mid = _merge(o_gla, p, h2, w_spatial, b_sp, wb0_b, wb1_b, wo_b)
    return _ffn(h_mid, norm_mlp_w.reshape(1, D_MODEL), wu_b, wd_b, final_w)


def kernel(x, norm_mix_w, w_in, w_alpha_up, b_alpha, gla_norm_w, gmlp_ln_w, gmlp_ln_b,
           w_spatial, b_spatial, b_gate, w_branch, w_out, norm_mlp_w, w_ff_up, w_ff_down,
           norm_final_w):
    batch, seq, d = x.shape
    depth = w_in.shape[0]
    assert d == D_MODEL and depth == 1 and seq % GLA_TB == 0
    assert w_in.shape[2] == D_IN
    h2 = x.reshape(batch * seq, d)
    y = _layer(h2, batch, seq, norm_mix_w[0], w_in[0], w_alpha_up[0], b_alpha[0],
               gla_norm_w[0], gmlp_ln_w[0], gmlp_ln_b[0], w_spatial[0], b_spatial[0],
               b_gate[0], w_branch[0], w_out[0], norm_mlp_w[0], w_ff_up[0], w_ff_down[0],
               norm_final_w.reshape(1, D_MODEL))
    return y.reshape(batch, seq, d)
```

```python
import math

import jax
import jax.numpy as jnp
from jax import lax
from jax.experimental import pallas as pl
from jax.experimental.pallas import tpu as pltpu

D_MODEL = 2048
CHUNK = 64
GLA_HEADS = 4
GLA_DK = D_MODEL // 8
GLA_DV = D_MODEL // 4
GLA_GATE_RANK = 16
GLA_TAU = 16.0
GMLP_BLOCK = 128
GMLP_GROUPS = 8
GMLP_DG = D_MODEL // GMLP_GROUPS
D_FF = 4 * D_MODEL
EPS = 1e-6
QK_W = GLA_HEADS * GLA_DK
V_W = GLA_HEADS * GLA_DV
D_IN = 2 * QK_W + 2 * V_W + GLA_GATE_RANK + 4 * D_MODEL

F32 = jnp.float32
BF16 = jnp.bfloat16

LANES = 128

WCOL_Q = 0
WCOL_K = QK_W
WCOL_V = 2 * QK_W
WCOL_R = WCOL_V + V_W
WCOL_GLR = WCOL_R + V_W
WCOL_GU = WCOL_GLR + GLA_GATE_RANK

COL_V = 0
COL_R = COL_V + V_W
COL_GU = COL_R + V_W
COL_GV = COL_GU + D_MODEL
COL_GATE = COL_GV + D_MODEL
COL_K = COL_GATE + 2 * D_MODEL
P_WIDTH = COL_K + QK_W

V7X_VMEM_BYTES = 64 * 1024 * 1024
VMEM_LIMIT = V7X_VMEM_BYTES - 8 * 1024 * 1024

QN_TM = 1024
QN_RC = 256
QN_SLOTS = 3
IN_TM = 2048
IN_TN = 1024
IN_RC = 512
IN_NC = 512
IN_UNROLL = 2
IN_SIDE_STEPS = 32
CAST_RC = 256
GLA_TB = 256
GLA_UNROLL = 4
MERGE_TM = 256
MERGE_SLOTS = 3
FFN_TM = 512
FFN_TF = 1024


def _sigmoid(x):
    return 0.5 * (1.0 + jnp.tanh(0.5 * x))


def _gelu_bf16(x):
    c = math.sqrt(2.0 / math.pi)
    inner = x * (c + (c * 0.044715) * (x * x))
    return (0.5 * x.astype(BF16)) * (1.0 + jnp.tanh(inner.astype(BF16)))


def _cast_transposed(wt_ref, wb_ref, n_rows):
    for rc in range(n_rows // CAST_RC):
        rows = pl.ds(rc * CAST_RC, CAST_RC)
        wb_ref[:, rc * CAST_RC:(rc + 1) * CAST_RC] = wt_ref[rows, :].T.astype(BF16)


def _qnorm_kernel(x_hbm, nw_ref, wq_ref, wg_ref, xn_ref, q_ref, glr_ref, wqg_ref, xbuf_ref, sem):
    i = pl.program_id(0)
    n = pl.num_programs(0)

    def x_copy(step, slot):
        return pltpu.make_async_copy(x_hbm.at[pl.ds(step * QN_TM, QN_TM), :],
                                     xbuf_ref.at[slot], sem.at[slot])

    @pl.when(i == 0)
    def _():
        for s in range(QN_SLOTS - 1):
            x_copy(s, s).start()

    ahead = i + (QN_SLOTS - 1)

    @pl.when(ahead < n)
    def _():
        x_copy(ahead, ahead % QN_SLOTS).start()

    @pl.when(i == 0)
    def _():
        _cast_transposed(wq_ref, wqg_ref, QK_W)
        wqg_ref[:, QK_W:] = wg_ref[...].T.astype(BF16)

    slot = i % QN_SLOTS
    x_copy(i, slot).wait()
    x_ref = xbuf_ref.at[slot]

    for c in range(QN_TM // QN_RC):
        rows = pl.ds(c * QN_RC, QN_RC)
        x = x_ref[rows, :]
        ms = jnp.mean(x * x, axis=-1, keepdims=True)
        xn = (x * lax.rsqrt(ms + EPS) * nw_ref[...]).astype(BF16)
        xn_ref[rows, :] = xn
        qg = jnp.dot(xn, wqg_ref[...], preferred_element_type=F32)
        q_ref[rows, :] = (qg[:, :QK_W] * (GLA_DK ** -0.5)).astype(BF16)
        glr_ref[rows, :] = qg[:, QK_W:].astype(BF16)


def _qnorm(x2, norm_w, w_in_t):
    m = x2.shape[0]
    single = pl.Buffered(1)
    return pl.pallas_call(
        _qnorm_kernel,
        grid=(m // QN_TM,),
        in_specs=[
            pl.BlockSpec(memory_space=pl.ANY),
            pl.BlockSpec((1, D_MODEL), lambda i: (0, 0)),
            pl.BlockSpec((QK_W, D_MODEL), lambda i: (WCOL_Q // QK_W, 0), pipeline_mode=single),
            pl.BlockSpec((LANES, D_MODEL), lambda i: (WCOL_GLR // LANES, 0), pipeline_mode=single),
        ],
        out_specs=[
            pl.BlockSpec((QN_TM, D_MODEL), lambda i: (i, 0)),
            pl.BlockSpec((QN_TM, QK_W), lambda i: (i, 0)),
            pl.BlockSpec((QN_TM, LANES), lambda i: (i, 0)),
        ],
        out_shape=[
            jax.ShapeDtypeStruct((m, D_MODEL), BF16),
            jax.ShapeDtypeStruct((m, QK_W), BF16),
            jax.ShapeDtypeStruct((m, LANES), BF16),
        ],
        scratch_shapes=[pltpu.VMEM((D_MODEL, QK_W + LANES), BF16),
                        pltpu.VMEM((QN_SLOTS, QN_TM, D_MODEL), F32),
                        pltpu.SemaphoreType.DMA((QN_SLOTS,))],
        compiler_params=pltpu.CompilerParams(
            dimension_semantics=("arbitrary",), vmem_limit_bytes=VMEM_LIMIT),
        name="qnorm",
    )(x2, norm_w, w_in_t, w_in_t)


_T_GU, _T_GV, _T_GATE, _T_K = (c // IN_TN for c in (COL_GU, COL_GV, COL_GATE, COL_K))
_N_T = P_WIDTH // IN_TN


def _in_proj_kernel(xn_ref, w_ref, bg_ref, lnw_ref, lnb_ref, wb0_ref, wb1_ref, wo_ref,
                    p_ref, wb0b_ref, wb1b_ref, wob_ref, wb_ref):
    t = pl.program_id(0)
    i = pl.program_id(1)

    @pl.when(i == 0)
    def _():
        _cast_transposed(w_ref, wb_ref, IN_TN)

    side_rows = wo_ref.shape[0] // (IN_TM // IN_RC)

    def run(epilogue):
        def chunk_body(c, carry):
            rows = pl.ds(pl.multiple_of(c * IN_RC, IN_RC), IN_RC)
            for n in range(IN_TN // IN_NC):
                cols = slice(n * IN_NC, (n + 1) * IN_NC)
                acc = jnp.dot(xn_ref[rows, :], wb_ref[:, cols], preferred_element_type=F32)
                p_ref[rows, cols] = epilogue(acc, cols).astype(BF16)
            wr = pl.ds(pl.multiple_of(c * side_rows, side_rows), side_rows)
            for src, dst in ((wb0_ref, wb0b_ref), (wb1_ref, wb1b_ref), (wo_ref, wob_ref)):
                dst[wr, :] = src[wr, :].astype(BF16)
            return carry

        lax.fori_loop(0, IN_TM // IN_RC, chunk_body, 0, unroll=IN_UNROLL)

    def epi_ln(acc, _):
        outs = []
        for g in range(IN_NC // GMLP_DG):
            cols = slice(g * GMLP_DG, (g + 1) * GMLP_DG)
            v = _gelu_bf16(acc[:, cols]).astype(F32)
            mu = jnp.mean(v, axis=-1, keepdims=True)
            vc = v - mu
            var = jnp.mean(vc * vc, axis=-1, keepdims=True)
            outs.append(vc * lax.rsqrt(var + EPS) * lnw_ref[...] + lnb_ref[...])
        return jnp.concatenate(outs, axis=-1)

    @pl.when(jnp.logical_or(t < _T_GU, t >= _T_K))
    def _():
        run(lambda a, _: a)

    @pl.when(jnp.logical_and(t >= _T_GU, t < _T_GV))
    def _():
        run(lambda a, _: _gelu_bf16(a))

    @pl.when(jnp.logical_and(t >= _T_GV, t < _T_GATE))
    def _():
        run(epi_ln)

    @pl.when(jnp.logical_and(t >= _T_GATE, t < _T_K))
    def _():
        branch = (t - _T_GATE) // (D_MODEL // IN_TN)
        run(lambda a, cols: _sigmoid((a + bg_ref[pl.ds(branch, 1), cols]).astype(BF16)))


def _in_proj(xn, w_in_t, b_gate, ln_w, ln_b, w_branch, w_out):
    m = xn.shape[0]
    n_i = m // IN_TM
    n_branch = b_gate.shape[0]
    n_col = D_MODEL // IN_TN
    n_gate_blocks = n_branch * n_col
    assert _N_T * n_i >= IN_SIDE_STEPS
    side_blk = lambda t, i: jnp.minimum(t * n_i + i, IN_SIDE_STEPS - 1)
    side_spec = pl.BlockSpec((D_MODEL // IN_SIDE_STEPS, D_MODEL),
                             lambda t, i: (side_blk(t, i), 0))
    side_br_spec = lambda n: pl.BlockSpec((None, D_MODEL // IN_SIDE_STEPS, D_MODEL),
                                          lambda t, i: (n, side_blk(t, i), 0))
    w_shape = jax.ShapeDtypeStruct((D_MODEL, D_MODEL), BF16)

    def w_rows(t, i):
        col = jnp.where(t < _T_GU, WCOL_V + t * IN_TN,
                        jnp.where(t < _T_K, WCOL_GU + (t - _T_GU) * IN_TN, WCOL_K))
        return (pl.multiple_of(col, GLA_GATE_RANK), 0)

    return pl.pallas_call(
        _in_proj_kernel,
        grid=(_N_T, n_i),
        in_specs=[
            pl.BlockSpec((IN_TM, D_MODEL), lambda t, i: (i, 0)),
            pl.BlockSpec((pl.Element(IN_TN), pl.Element(D_MODEL)), w_rows),
            pl.BlockSpec((n_branch, IN_TN),
                         lambda t, i: (0, jnp.clip(t - _T_GATE, 0, n_gate_blocks - 1) % n_col)),
            pl.BlockSpec((1, GMLP_DG), lambda t, i: (0, 0)),
            pl.BlockSpec((1, GMLP_DG), lambda t, i: (0, 0)),
            side_br_spec(0), side_br_spec(1), side_spec,
        ],
        out_specs=[pl.BlockSpec((IN_TM, IN_TN), lambda t, i: (i, t)),
                   side_spec, side_spec, side_spec],
        out_shape=[jax.ShapeDtypeStruct((m, P_WIDTH), BF16), w_shape, w_shape, w_shape],
        scratch_shapes=[pltpu.VMEM((D_MODEL, IN_TN), BF16)],
        compiler_params=pltpu.CompilerParams(
            dimension_semantics=("arbitrary", "arbitrary"), vmem_limit_bytes=VMEM_LIMIT),
        name="in_proj",
    )(xn, w_in_t, b_gate, ln_w, ln_b, w_branch, w_branch, w_out)


def _gla_kernel(q_ref, k_ref, vr_ref, glr_ref, wup_ref, ba_ref, gnw_ref,
                wu_ref, wd_ref, o_ref, wub_ref, wdb_ref, state_ref, z_ref):
    t = pl.program_id(1)

    @pl.when(t == 0)
    def _():
        state_ref[...] = jnp.zeros_like(state_ref)

    z_ref[...] = jnp.dot(glr_ref[:, :GLA_GATE_RANK], wup_ref[...].astype(BF16),
                         preferred_element_type=F32) + ba_ref[...]

    row = lax.broadcasted_iota(jnp.int32, (CHUNK, CHUNK), 0)
    col = lax.broadcasted_iota(jnp.int32, (CHUNK, CHUNK), 1)
    tri = (row >= col).astype(BF16)
    tri2 = jnp.concatenate([tri, tri], axis=1)

    n_chunks = GLA_TB // CHUNK
    wu_rows = wu_ref.shape[0] // n_chunks
    wd_rows = wd_ref.shape[0] // n_chunks

    def chunk_body(c, carry):
        for src, dst, n in ((wu_ref, wub_ref, wu_rows), (wd_ref, wdb_ref, wd_rows)):
            wr = pl.ds(pl.multiple_of(c * n, n), n)
            dst[wr, :] = src[wr, :].astype(BF16)

        rows = pl.ds(pl.multiple_of(c * CHUNK, CHUNK), CHUNK)
        z = z_ref[rows, :]
        la = (jnp.minimum(z, 0.0) - jnp.log(1.0 + jnp.exp(-jnp.abs(z)))) * (1.0 / GLA_TAU)
        la_hi = la.astype(BF16)
        la_lo = (la - la_hi.astype(F32)).astype(BF16)
        lcum = jnp.dot(tri2, jnp.concatenate([la_hi, la_lo], axis=0),
                       preferred_element_type=F32)
        l_end = lcum[CHUNK - 1:CHUNK, :]
        k_dec = (k_ref[rows, :].astype(F32) * jnp.exp(l_end - lcum)).astype(BF16)
        decay = jnp.exp(l_end)
        q = q_ref[rows, :]
        v = vr_ref[rows, :V_W]
        r = vr_ref[rows, V_W:]
        rs = r * _sigmoid(r)
        for h in range(GLA_HEADS):
            kc = slice(h * GLA_DK, (h + 1) * GLA_DK)
            vc = slice(h * GLA_DV, (h + 1) * GLA_DV)
            kv_t = lax.dot_general(v[:, vc], k_dec[:, kc], (((0,), (0,)), ((), ())),
                                   preferred_element_type=F32)
            s = state_ref[h] * decay[:, kc] + kv_t
            state_ref[h] = s
            o = lax.dot_general(q[:, kc], s.astype(BF16), (((1,), (1,)), ((), ())),
                                preferred_element_type=F32)
            ms = jnp.mean(o * o, axis=-1, keepdims=True)
            o = o * lax.rsqrt(ms + EPS) * gnw_ref[...] * rs[:, vc].astype(F32)
            o_ref[rows, vc] = o.astype(BF16)
        return carry

    lax.fori_loop(0, n_chunks, chunk_body, 0, unroll=GLA_UNROLL)


def _gla(q, p, glr, wup, b_alpha, gnw, w_ff_up, w_ff_down, batch, seq):
    m = p.shape[0]
    nt = seq // GLA_TB
    n_steps = batch * nt
    rowblk = lambda b, t: b * nt + t
    wu_spec = pl.BlockSpec((D_MODEL // n_steps, D_FF), lambda b, t: (rowblk(b, t), 0))
    wd_spec = pl.BlockSpec((D_FF // n_steps, D_MODEL), lambda b, t: (rowblk(b, t), 0))
    return pl.pallas_call(
        _gla_kernel,
        grid=(batch, nt),
        in_specs=[
            pl.BlockSpec((GLA_TB, QK_W), lambda b, t: (rowblk(b, t), 0)),
            pl.BlockSpec((GLA_TB, QK_W), lambda b, t: (rowblk(b, t), COL_K // QK_W)),
            pl.BlockSpec((GLA_TB, 2 * V_W), lambda b, t: (rowblk(b, t), COL_V // (2 * V_W))),
            pl.BlockSpec((GLA_TB, LANES), lambda b, t: (rowblk(b, t), 0)),
            pl.BlockSpec((GLA_GATE_RANK, QK_W), lambda b, t: (0, 0)),
            pl.BlockSpec((1, QK_W), lambda b, t: (0, 0)),
            pl.BlockSpec((1, GLA_DV), lambda b, t: (0, 0)),
            wu_spec, wd_spec,
        ],
        out_specs=[pl.BlockSpec((GLA_TB, V_W), lambda b, t: (rowblk(b, t), 0)),
                   wu_spec, wd_spec],
        out_shape=[jax.ShapeDtypeStruct((m, V_W), BF16),
                   jax.ShapeDtypeStruct((D_MODEL, D_FF), BF16),
                   jax.ShapeDtypeStruct((D_FF, D_MODEL), BF16)],
        scratch_shapes=[
            pltpu.VMEM((GLA_HEADS, GLA_DV, GLA_DK), F32),
            pltpu.VMEM((GLA_TB, QK_W), F32),
        ],
        compiler_params=pltpu.CompilerParams(
            dimension_semantics=("arbitrary", "arbitrary"), vmem_limit_bytes=VMEM_LIMIT),
        name="gla",
    )(q, p, p, glr, wup, b_alpha, gnw, w_ff_up, w_ff_down)


def _merge_kernel(og_hbm, p_hbm, x_hbm, ws_ref, bs_ref, wb0_ref, wb1_ref, wo_ref, h_ref,
                  gm_ref, og_buf, guv_buf, gate_buf, x_buf, sem):
    i = pl.program_id(0)
    n = pl.num_programs(0)

    def tile_copies(step, slot):
        rows = pl.ds(step * MERGE_TM, MERGE_TM)
        return (
            pltpu.make_async_copy(og_hbm.at[rows, :], og_buf.at[slot], sem.at[0, slot]),
            pltpu.make_async_copy(p_hbm.at[rows, pl.ds(COL_GU, 2 * D_MODEL)],
                                  guv_buf.at[slot], sem.at[1, slot]),
            pltpu.make_async_copy(p_hbm.at[rows, pl.ds(COL_GATE, 2 * D_MODEL)],
                                  gate_buf.at[slot], sem.at[2, slot]),
            pltpu.make_async_copy(x_hbm.at[rows, :], x_buf.at[slot], sem.at[3, slot]),
        )

    @pl.when(i == 0)
    def _():
        for s in range(MERGE_SLOTS - 1):
            for cp in tile_copies(s, s):
                cp.start()

    ahead = i + (MERGE_SLOTS - 1)

    @pl.when(ahead < n)
    def _():
        for cp in tile_copies(ahead, ahead % MERGE_SLOTS):
            cp.start()

    slot = i % MERGE_SLOTS
    for cp in tile_copies(i, slot):
        cp.wait()
    og_ref = og_buf.at[slot]
    guv_ref = guv_buf.at[slot]
    gate_ref = gate_buf.at[slot]
    x_ref = x_buf.at[slot]

    pos_t = lax.broadcasted_iota(jnp.int32, (GMLP_BLOCK, GMLP_BLOCK), 0) // CHUNK
    pos_s = lax.broadcasted_iota(jnp.int32, (GMLP_BLOCK, GMLP_BLOCK), 1) // CHUNK
    causal = pos_t >= pos_s
    for g in range(GMLP_GROUPS):
        w = jnp.where(causal, ws_ref[g], 0.0).astype(BF16)
        cols = slice(g * GMLP_DG, (g + 1) * GMLP_DG)
        vcols = slice(D_MODEL + g * GMLP_DG, D_MODEL + (g + 1) * GMLP_DG)
        for blk in range(MERGE_TM // GMLP_BLOCK):
            rows = pl.ds(blk * GMLP_BLOCK, GMLP_BLOCK)
            mixed = jnp.dot(w, guv_ref[rows, vcols], preferred_element_type=F32) + bs_ref[g]
            gm_ref[rows, cols] = (guv_ref[rows, cols].astype(F32) * mixed).astype(BF16)

    d0 = jnp.dot(og_ref[...], wb0_ref[...], preferred_element_type=F32)
    d1 = jnp.dot(gm_ref[...], wb1_ref[...], preferred_element_type=F32)
    mixed = (gate_ref[:, :D_MODEL].astype(F32) * d0
             + gate_ref[:, D_MODEL:].astype(F32) * d1).astype(BF16)
    h_ref[...] = x_ref[...] + jnp.dot(mixed, wo_ref[...], preferred_element_type=F32)


def _merge(o_gla, p, x2, w_spatial, b_sp, wb0, wb1, wo):
    m = x2.shape[0]
    const2 = lambda i: (0, 0)
    const3 = lambda i: (0, 0, 0)
    single = pl.Buffered(1)
    return pl.pallas_call(
        _merge_kernel,
        grid=(m // MERGE_TM,),
        in_specs=[
            pl.BlockSpec(memory_space=pl.ANY),
            pl.BlockSpec(memory_space=pl.ANY),
            pl.BlockSpec(memory_space=pl.ANY),
            pl.BlockSpec((GMLP_GROUPS, GMLP_BLOCK, GMLP_BLOCK), const3, pipeline_mode=single),
            pl.BlockSpec((GMLP_GROUPS, GMLP_BLOCK, GMLP_DG), const3, pipeline_mode=single),
            pl.BlockSpec((V_W, D_MODEL), const2, pipeline_mode=single),
            pl.BlockSpec((V_W, D_MODEL), const2, pipeline_mode=single),
            pl.BlockSpec((D_MODEL, D_MODEL), const2, pipeline_mode=single),
        ],
        out_specs=pl.BlockSpec((MERGE_TM, D_MODEL), lambda i: (i, 0)),
        out_shape=jax.ShapeDtypeStruct((m, D_MODEL), F32),
        scratch_shapes=[pltpu.VMEM((MERGE_TM, D_MODEL), BF16),
                        pltpu.VMEM((MERGE_SLOTS, MERGE_TM, V_W), BF16),
                        pltpu.VMEM((MERGE_SLOTS, MERGE_TM, 2 * D_MODEL), BF16),
                        pltpu.VMEM((MERGE_SLOTS, MERGE_TM, 2 * D_MODEL), BF16),
                        pltpu.VMEM((MERGE_SLOTS, MERGE_TM, D_MODEL), F32),
                        pltpu.SemaphoreType.DMA((4, MERGE_SLOTS))],
        compiler_params=pltpu.CompilerParams(
            dimension_semantics=("arbitrary",), vmem_limit_bytes=VMEM_LIMIT),
        name="merge",
    )(o_gla, p, x2, w_spatial, b_sp, wb0, wb1, wo)


def _ffn_kernel(h_ref, nw_ref, wu_ref, wd_ref, fw_ref, y_ref, hn_ref):
    f = pl.program_id(1)

    @pl.when(f == 0)
    def _():
        h = h_ref[...]
        ms = jnp.mean(h * h, axis=-1, keepdims=True)
        hn_ref[...] = (h * lax.rsqrt(ms + EPS) * nw_ref[...]).astype(BF16)
        y_ref[...] = h

    up = jnp.dot(hn_ref[...], wu_ref[...], preferred_element_type=F32)
    up = jnp.square(jnp.maximum(up, 0.0)).astype(BF16)
    y_ref[...] += jnp.dot(up, wd_ref[...], preferred_element_type=F32)

    @pl.when(f == pl.num_programs(1) - 1)
    def _():
        h2 = y_ref[...]
        ms = jnp.mean(h2 * h2, axis=-1, keepdims=True)
        y_ref[...] = h2 * lax.rsqrt(ms + EPS) * fw_ref[...]


def _ffn(h, norm_w, wu, wd, final_w):
    m = h.shape[0]
    return pl.pallas_call(
        _ffn_kernel,
        grid=(m // FFN_TM, D_FF // FFN_TF),
        in_specs=[
            pl.BlockSpec((FFN_TM, D_MODEL), lambda i, f: (i, 0)),
            pl.BlockSpec((1, D_MODEL), lambda i, f: (0, 0)),
            pl.BlockSpec((D_MODEL, FFN_TF), lambda i, f: (0, f)),
            pl.BlockSpec((FFN_TF, D_MODEL), lambda i, f: (f, 0)),
            pl.BlockSpec((1, D_MODEL), lambda i, f: (0, 0)),
        ],
        out_specs=pl.BlockSpec((FFN_TM, D_MODEL), lambda i, f: (i, 0)),
        out_shape=jax.ShapeDtypeStruct((m, D_MODEL), F32),
        scratch_shapes=[pltpu.VMEM((FFN_TM, D_MODEL), BF16)],
        compiler_params=pltpu.CompilerParams(
            dimension_semantics=("arbitrary", "arbitrary"), vmem_limit_bytes=VMEM_LIMIT),
        name="ffn",
    )(h, norm_w, wu, wd, final_w)


def _layer(h2, batch, seq, norm_mix_w, w_in, w_alpha_up, b_alpha, gla_norm_w, gmlp_ln_w,
           gmlp_ln_b, w_spatial, b_spatial, b_gate, w_branch, w_out, norm_mlp_w, w_ff_up,
           w_ff_down, final_w):
    w_in_t = w_in.T
    xn, q, glr = _qnorm(h2, norm_mix_w.reshape(1, D_MODEL), w_in_t)
    p, wb0_b, wb1_b, wo_b = _in_proj(
        xn, w_in_t, b_gate, gmlp_ln_w.reshape(1, GMLP_DG),
        gmlp_ln_b.reshape(1, GMLP_DG), w_branch, w_out)
    o_gla, wu_b, wd_b = _gla(
        q, p, glr, w_alpha_up, b_alpha.reshape(1, QK_W),
        gla_norm_w.reshape(1, GLA_DV), w_ff_up, w_ff_down, batch, seq)
    b_sp = jnp.broadcast_to(b_spatial[:, :, None], (GMLP_GROUPS, GMLP_BLOCK, GMLP_DG))
    h_mid = _merge(o_gla, p, h2, w_spatial, b_sp, wb0_b, wb1_b, wo_b)
    return _ffn(h_mid, norm_mlp_w.reshape(1, D_MODEL), wu_b, wd_b, final_w)


def kernel(x, norm_mix_w, w_in, w_alpha_up, b_alpha, gla_norm_w, gmlp_ln_w, gmlp_ln_b,
           w_spatial, b_spatial, b_gate, w_branch, w_out, norm_mlp_w, w_ff_up, w_ff_down,
           norm_final_w):
    batch, seq, d = x.shape
    depth = w_in.shape[0]
    assert d == D_MODEL and depth == 1 and seq % GLA_TB == 0
    assert w_in.shape[2] == D_IN
    h2 = x.reshape(batch * seq, d)
    y = _layer(h2, batch, seq, norm_mix_w[0], w_in[0], w_alpha_up[0], b_alpha[0],
               gla_norm_w[0], gmlp_ln_w[0], gmlp_ln_b[0], w_spatial[0], b_spatial[0],
               b_gate[0], w_branch[0], w_out[0], norm_mlp_w[0], w_ff_up[0], w_ff_down[0],
               norm_final_w.reshape(1, D_MODEL))
    return y.reshape(batch, seq, d)
```
